```python
import jax, jax.numpy as jnp
from jax import lax
import numpy as np

D_MODEL = 1024
BATCH = 8
SEQ = 2048
DEPTH = 4

POOL_WINDOWS = (2, 4, 8, 16)
N_POOL_GROUPS = len(POOL_WINDOWS)
POOL_WIDTH = D_MODEL
POOL_GROUP = POOL_WIDTH // N_POOL_GROUPS
SSD_EXPAND = 2
SSD_INNER = SSD_EXPAND * D_MODEL
SSD_HEAD_DIM = 64
SSD_HEADS = SSD_INNER // SSD_HEAD_DIM
SSD_GROUPS = 4
SSD_HEADS_PER_GROUP = SSD_HEADS // SSD_GROUPS
SSD_STATE = 128
SSD_CONV = 5
SSD_CHUNK = 128
SSD_CONV_CH = SSD_INNER + 2 * SSD_GROUPS * SSD_STATE
N_BRANCHES = 2
D_FF = 2816
FFN_CONV = 3
ALPHA = (2 * DEPTH) ** 0.25
BETA = (8 * DEPTH) ** -0.25
LN_EPS = 1e-5
RMS_EPS = 1e-5
IN_SPLITS = (POOL_WIDTH, POOL_WIDTH + SSD_INNER, POOL_WIDTH + SSD_INNER + SSD_CONV_CH,
             POOL_WIDTH + SSD_INNER + SSD_CONV_CH + 2 * SSD_HEADS)
IN_COLS = IN_SPLITS[-1] + N_BRANCHES * D_MODEL

kernel_name = 'hybrid_pool_ssd_convffn_encoder'


def layer_norm(x, g, b):
    xf = x.astype(jnp.float32)
    mu = jnp.mean(xf, axis=-1, keepdims=True)
    var = jnp.mean(jnp.square(xf - mu), axis=-1, keepdims=True)
    return ((xf - mu) * lax.rsqrt(var + LN_EPS) * g + b).astype(x.dtype)


def dwconv_centred(x, w, b):
    pad = w.shape[0] // 2
    y = lax.conv_general_dilated(x, w[:, None, :], window_strides=(1,), padding=[(pad, pad)],
                                 dimension_numbers=('NWC', 'WIO', 'NWC'),
                                 feature_group_count=x.shape[-1])
    return y + b


def multiscale_pool(u, w_map, scale):
    bsz, s, _ = u.shape
    ug = u.astype(jnp.float32).reshape(bsz, s, N_POOL_GROUPS, POOL_GROUP)
    t = np.arange(s)
    groups = []
    for gi, w in enumerate(POOL_WINDOWS):
        half = w // 2
        v = ug[:, :, gi]
        cs = jnp.cumsum(jnp.pad(v, ((0, 0), (half + 1, half), (0, 0))), axis=1)
        cnt = np.minimum(t + half - 1, s - 1) - np.maximum(t - half, 0) + 1
        mean = (cs[:, w:w + s] - cs[:, :s]) / jnp.asarray(cnt, jnp.float32)[:, None]
        groups.append(mean - v)
    pooled = jnp.stack(groups, axis=2).astype(u.dtype)
    out = jnp.einsum('bsgc,gcd->bsgd', pooled, w_map).reshape(bsz, s, POOL_WIDTH)
    return out * scale


def ssd_scan(xh, dt, a, bmat, cmat):
    f32 = jnp.float32
    bsz, s = xh.shape[0], xh.shape[1]
    nc = s // SSD_CHUNK
    shp = (bsz, nc, SSD_CHUNK, SSD_GROUPS)
    x = (xh.astype(f32) * dt[..., None]).reshape(*shp, SSD_HEADS_PER_GROUP, SSD_HEAD_DIM)
    bc = bmat.astype(f32).reshape(*shp, SSD_STATE)
    cc = cmat.astype(f32).reshape(*shp, SSD_STATE)
    adt = (dt * a).reshape(*shp, SSD_HEADS_PER_GROUP)
    a_cum = jnp.cumsum(jnp.moveaxis(adt, 2, -1), axis=-1)
    seg = a_cum[..., :, None] - a_cum[..., None, :]
    lower = np.tril(np.ones((SSD_CHUNK, SSD_CHUNK), dtype=bool))
    decay = jnp.exp(jnp.where(lower, seg, -jnp.inf))
    scores = jnp.einsum('bclgn,bcsgn->bcgls', cc, bc)
    y_diag = jnp.einsum('bcghls,bcsghp->bclghp', scores[:, :, :, None] * decay, x)

    def step(h, inp):
        b_c, c_c, x_c, ac_c = inp
        y_off = jnp.einsum('blgn,bghpn,bghl->blghp', c_c, h, jnp.exp(ac_c))
        decay_end = jnp.exp(ac_c[..., -1:] - ac_c)
        new_state = jnp.einsum('blgn,bghl,blghp->bghpn', b_c, decay_end, x_c)
        h = h * jnp.exp(ac_c[..., -1])[..., None, None] + new_state
        return h, y_off

    h0 = jnp.zeros((bsz, SSD_GROUPS, SSD_HEADS_PER_GROUP, SSD_HEAD_DIM, SSD_STATE), f32)
    xs = (jnp.moveaxis(bc, 1, 0), jnp.moveaxis(cc, 1, 0), jnp.moveaxis(x, 1, 0),
          jnp.moveaxis(a_cum, 1, 0))
    _, y_off = lax.scan(step, h0, xs)
    y = y_diag + jnp.moveaxis(y_off, 0, 1)
    return y.reshape(bsz, s, SSD_HEADS, SSD_HEAD_DIM)


def bidirectional_ssd(xh, dt_raw, a_log, dt_bias, bmat, cmat, d_skip):
    bsz, s = xh.shape[0], xh.shape[1]
    dt = jax.nn.softplus(dt_raw.astype(jnp.float32).reshape(bsz, s, 2, SSD_HEADS)
                         + dt_bias.astype(jnp.float32))
    a = -jnp.exp(a_log.astype(jnp.float32))
    y_f = ssd_scan(xh, dt[:, :, 0], a[0], bmat, cmat)
    fl = lambda t: jnp.flip(t, axis=1)
    y_b = fl(ssd_scan(fl(xh), fl(dt[:, :, 1]), a[1], fl(bmat), fl(cmat)))
    y = y_f + y_b + d_skip.astype(jnp.float32)[:, None] * xh.astype(jnp.float32)
    return y.astype(xh.dtype)


def gated_rmsnorm(y, z, g):
    bsz, s, _ = y.shape
    v = (y.astype(jnp.float32) * jax.nn.silu(z.astype(jnp.float32))).reshape(bsz, s, SSD_GROUPS, -1)
    v = v * lax.rsqrt(jnp.mean(v * v, axis=-1, keepdims=True) + RMS_EPS)
    return (v.reshape(bsz, s, SSD_INNER) * g).astype(y.dtype)


def mixer_sublayer(x, w_in, pool_w, pool_scale, ssd_conv_w, ssd_conv_b, a_log, dt_bias,
                   d_skip, ssd_norm_g, w_ssd_proj, w_out):
    bsz, s, _ = x.shape
    proj = x @ w_in
    u_pool, z, xbc, dt_raw, gate_logits = jnp.split(proj, list(IN_SPLITS), axis=-1)
    pool_out = multiscale_pool(u_pool, pool_w, pool_scale)
    xbc = jax.nn.silu(dwconv_centred(xbc, ssd_conv_w, ssd_conv_b))
    gn = SSD_GROUPS * SSD_STATE
    xs, bm, cm = jnp.split(xbc, [SSD_INNER, SSD_INNER + gn], axis=-1)
    xh = xs.reshape(bsz, s, SSD_HEADS, SSD_HEAD_DIM)
    bm = bm.reshape(bsz, s, SSD_GROUPS, SSD_STATE)
    cm = cm.reshape(bsz, s, SSD_GROUPS, SSD_STATE)
    y = bidirectional_ssd(xh, dt_raw, a_log, dt_bias, bm, cm, d_skip)
    y = gated_rmsnorm(y.reshape(bsz, s, SSD_INNER), z, ssd_norm_g)
    ssd_out = y @ w_ssd_proj
    gates = jax.nn.sigmoid(gate_logits.astype(jnp.float32)).astype(x.dtype)
    gates = gates.reshape(bsz, s, N_BRANCHES, D_MODEL)
    merged = gates[:, :, 0] * pool_out + gates[:, :, 1] * ssd_out
    return merged @ w_out


def conv_ffn(x, w_up, ffn_conv_w, ffn_conv_b, w_down):
    h = dwconv_centred(x @ w_up, ffn_conv_w, ffn_conv_b)
    gate, val = jnp.split(h, 2, axis=-1)
    return (jax.nn.gelu(gate, approximate=False) * val) @ w_down


def setup_inputs(seed: int = 0) -> dict:
    key = jax.random.key(seed)
    ks = jax.random.split(key, 21)
    nrm = lambda k, shp, sc: jax.random.normal(k, shp, jnp.float32) * sc
    L = DEPTH
    dt0 = jnp.exp(jax.random.uniform(ks[8], (L, 2, SSD_HEADS), jnp.float32,
                                     np.log(1e-3), np.log(1e-1)))
    return {
        'x': nrm(ks[0], (BATCH, SEQ, D_MODEL), 1.0),
        'w_in': nrm(ks[1], (L, D_MODEL, IN_COLS), D_MODEL ** -0.5),
        'pool_w': nrm(ks[2], (L, N_POOL_GROUPS, POOL_GROUP, POOL_GROUP), POOL_GROUP ** -0.5),
        'pool_scale': 1.0 + nrm(ks[3], (L, POOL_WIDTH), 0.02),
        'ssd_conv_w': nrm(ks[4], (L, SSD_CONV, SSD_CONV_CH), SSD_CONV ** -0.5),
        'ssd_conv_b': nrm(ks[5], (L, SSD_CONV_CH), 0.02),
        'a_log': jnp.log(jax.random.uniform(ks[6], (L, 2, SSD_HEADS), jnp.float32, 1.0, 16.0)),
        'dt_bias': dt0 + jnp.log(-jnp.expm1(-dt0)),
        'd_skip': 1.0 + nrm(ks[7], (L, SSD_HEADS), 0.02),
        'ssd_norm_g': 1.0 + nrm(ks[9], (L, SSD_INNER), 0.02),
        'w_ssd_proj': nrm(ks[10], (L, SSD_INNER, D_MODEL), SSD_INNER ** -0.5),
        'w_out': nrm(ks[11], (L, D_MODEL, D_MODEL), BETA * D_MODEL ** -0.5),
        'ln1_g': 1.0 + nrm(ks[12], (L, D_MODEL), 0.02),
        'ln1_b': nrm(ks[13], (L, D_MODEL), 0.02),
        'w_up': nrm(ks[14], (L, D_MODEL, 2 * D_FF), D_MODEL ** -0.5),
        'ffn_conv_w': nrm(ks[15], (L, FFN_CONV, 2 * D_FF), FFN_CONV ** -0.5),
        'ffn_conv_b': nrm(ks[16], (L, 2 * D_FF), 0.02),
        'w_down': nrm(ks[17], (L, D_FF, D_MODEL), BETA * D_FF ** -0.5),
        'ln2_g': 1.0 + nrm(ks[18], (L, D_MODEL), 0.02),
        'ln2_b': nrm(ks[19], (L, D_MODEL), 0.02),
    }


def reference(x, w_in, pool_w, pool_scale, ssd_conv_w, ssd_conv_b, a_log, dt_bias, d_skip,
              ssd_norm_g, w_ssd_proj, w_out, ln1_g, ln1_b, w_up, ffn_conv_w, ffn_conv_b,
              w_down, ln2_g, ln2_b):
    for i in range(DEPTH):
        mix = mixer_sublayer(x, w_in[i], pool_w[i], pool_scale[i], ssd_conv_w[i], ssd_conv_b[i],
                             a_log[i], dt_bias[i], d_skip[i], ssd_norm_g[i], w_ssd_proj[i],
                             w_out[i])
        x = layer_norm(ALPHA * x + mix, ln1_g[i], ln1_b[i])
        ffn = conv_ffn(x, w_up[i], ffn_conv_w[i], ffn_conv_b[i], w_down[i])
        x = layer_norm(ALPHA * x + ffn, ln2_g[i], ln2_b[i])
    return x
```

```python
import functools

import numpy as np
import jax
import jax.numpy as jnp
from jax import lax
from jax.experimental import pallas as pl
from jax.experimental.pallas import tpu as pltpu

F32 = jnp.float32
BF16 = jnp.bfloat16

POOL_WINDOWS = (2, 4, 8, 16)
N_POOL_GROUPS = 4
SSD_HEAD_DIM = 64
SSD_GROUPS = 4
SSD_HEADS_PER_GROUP = 8
SSD_HEADS = SSD_GROUPS * SSD_HEADS_PER_GROUP
SSD_STATE = 128
SSD_CHUNK = 128
LN_EPS = 1e-5
RMS_EPS = 1e-5

LANES = 128
SUBLANES = 8
VMEM_LIMIT_BYTES = 56 * 1024 * 1024

CONV_PAD_ROWS = SUBLANES
POOL_BLOCK = 256
POOL_HALO = 16
HEAD_GROUP_LANES = SSD_HEADS_PER_GROUP * SSD_HEAD_DIM
DIRS_X_HEADS = 2 * SSD_HEADS_PER_GROUP


def _cparams(*sem):
    return pltpu.CompilerParams(dimension_semantics=sem, vmem_limit_bytes=VMEM_LIMIT_BYTES)


def _sigmoid(v):
    return 1.0 / (1.0 + jnp.exp(-v))


def _softplus(v):
    return jnp.maximum(v, 0.0) + jnp.log1p(jnp.exp(-jnp.abs(v)))


def _dot(a, b):
    return jnp.dot(a, b, preferred_element_type=F32)


def _split3(v):
    hi = v.astype(BF16)
    r1 = v - hi.astype(F32)
    mid = r1.astype(BF16)
    lo = (r1 - mid.astype(F32)).astype(BF16)
    return hi, mid, lo


def _dot_exact_rhs(m, v):
    hi, mid, lo = _split3(v)
    return _dot(m, hi) + _dot(m, mid) + _dot(m, lo)


def _dot_exact_lhs(v, m):
    hi, mid, lo = _split3(v)
    return _dot(hi, m) + _dot(mid, m) + _dot(lo, m)


def _layer_norm(r, g, b):
    mu = jnp.mean(r, axis=-1, keepdims=True)
    d = r - mu
    var = jnp.mean(d * d, axis=-1, keepdims=True)
    return d * lax.rsqrt(var + LN_EPS) * g + b


def _dt_kernel(x_ref, wdt_ref, wdtT_ref, brow_ref, bcol_ref, alrow_ref, alcol_ref,
               pa_ref, pe_ref, pw_ref, rows_ref, acol_ref, colfac_ref, *, n_chunks):
    L = SSD_CHUNK
    x = x_ref[0]
    dt_c = _softplus(_dot(x, wdt_ref[...]) + brow_ref[...])
    dt_r = _softplus(lax.dot_general(wdtT_ref[...], x, (((1,), (1,)), ((), ())),
                                     preferred_element_type=F32) + bcol_ref[...])
    adt_c = dt_c * (-jnp.exp(alrow_ref[...]))
    adt_r = dt_r * (-jnp.exp(alcol_ref[...]))
    ri = lax.broadcasted_iota(jnp.int32, (L, L), 0)
    ci = lax.broadcasted_iota(jnp.int32, (L, L), 1)
    lower = (ri >= ci).astype(BF16)
    upper = (ri <= ci).astype(BF16)
    nh = dt_c.shape[1]
    isb_row = (lax.broadcasted_iota(jnp.int32, (1, nh), 1) // SSD_HEADS_PER_GROUP) % 2 == 1
    isb_col = (lax.broadcasted_iota(jnp.int32, (nh, 1), 0) // SSD_HEADS_PER_GROUP) % 2 == 1
    for c in range(n_chunks):
        sl = slice(c * L, (c + 1) * L)
        a_c = adt_c[sl]
        cum_c = jnp.where(isb_row, _dot_exact_rhs(upper, a_c), _dot_exact_rhs(lower, a_c))
        tot = jnp.where(isb_row, cum_c[0:1], cum_c[L - 1:L])
        e_a = jnp.exp(cum_c)
        w_in = dt_c[sl] * jnp.exp(tot - cum_c)
        e_hi = e_a.astype(BF16)
        e_lo = (e_a - e_hi.astype(F32)).astype(BF16)
        w_hi = w_in.astype(BF16)
        w_lo = (w_in - w_hi.astype(F32)).astype(BF16)
        for g in range(SSD_GROUPS):
            acol_ref[0, g, sl, :] = _dot_exact_lhs(cum_c, pa_ref[g])
            colfac_ref[0, g, sl, :] = (_dot(e_hi, pe_ref[0, g]) + _dot(w_hi, pw_ref[0, g])
                                       + _dot(e_lo, pe_ref[1, g]) + _dot(w_lo, pw_ref[1, g])
                                       ).astype(BF16)
        a_r = adt_r[:, sl]
        cum_r = jnp.where(isb_col, _dot_exact_lhs(a_r, lower), _dot_exact_lhs(a_r, upper))
        rows_ref[0, c, 0] = cum_r
        rows_ref[0, c, 1] = dt_r[:, sl]


def _dt_call(xb, wdt, wdtT, brow, bcol, alrow, alcol, pa, pe, pw):
    bsz, s, d = xb.shape
    nc = s // SSD_CHUNK
    nh = 2 * SSD_HEADS
    const = lambda *shape: pl.BlockSpec(shape, lambda b: (0,) * len(shape))
    return pl.pallas_call(
        functools.partial(_dt_kernel, n_chunks=nc),
        grid=(bsz,),
        in_specs=[pl.BlockSpec((1, s, d), lambda b: (b, 0, 0)),
                  const(d, nh), const(nh, d), const(1, nh), const(nh, 1), const(1, nh), const(nh, 1),
                  const(SSD_GROUPS, nh, LANES), const(2, SSD_GROUPS, nh, LANES),
                  const(2, SSD_GROUPS, nh, LANES)],
        out_specs=[pl.BlockSpec((1, nc, 2, nh, SSD_CHUNK), lambda b: (b, 0, 0, 0, 0)),
                   pl.BlockSpec((1, SSD_GROUPS, s, LANES), lambda b: (b, 0, 0, 0)),
                   pl.BlockSpec((1, SSD_GROUPS, s, LANES), lambda b: (b, 0, 0, 0))],
        out_shape=[jax.ShapeDtypeStruct((bsz, nc, 2, nh, SSD_CHUNK), F32),
                   jax.ShapeDtypeStruct((bsz, SSD_GROUPS, s, LANES), F32),
                   jax.ShapeDtypeStruct((bsz, SSD_GROUPS, s, LANES), BF16)],
        compiler_params=_cparams("parallel"),
        name="dt",
    )(xb, wdt, wdtT, brow, bcol, alrow, alcol, pa, pe, pw)


def _pool_kernel(x_ref, w_ref, wmap_ref, scale_ref, o_ref, u_ref, ub_ref, *, seq):
    g = pl.program_id(1)
    half = jnp.left_shift(1, g)
    u = _dot(x_ref[0], w_ref[...])
    u_ref[...] = u
    ub_ref[...] = u.astype(BF16)
    blk = min(POOL_BLOCK, seq)
    win = min(seq, blk + 2 * POOL_HALO)
    for r in range(seq // blk):
        start = max(0, min(r * blk - POOL_HALO, seq - win))
        t = r * blk + lax.broadcasted_iota(jnp.int32, (blk, win), 0)
        j = start + lax.broadcasted_iota(jnp.int32, (blk, win), 1)
        band = ((j >= t - half) & (j <= t + half - 1)).astype(BF16)
        ssum = _dot(band, ub_ref[start:start + win, :])
        tq = r * blk + lax.broadcasted_iota(jnp.int32, ssum.shape, 0)
        cnt = jnp.minimum(tq + half - 1, seq - 1) - jnp.maximum(tq - half, 0) + 1
        pooled = ssum / cnt.astype(F32) - u_ref[r * blk:(r + 1) * blk, :]
        out = _dot(pooled.astype(BF16), wmap_ref[0]) * scale_ref[...]
        o_ref[0, r * blk:(r + 1) * blk, :] = out.astype(o_ref.dtype)


def _pool_call(xb, w_pool, wmap, scale):
    bsz, s, d = xb.shape
    pg = w_pool.shape[1] // N_POOL_GROUPS
    return pl.pallas_call(
        functools.partial(_pool_kernel, seq=s),
        grid=(bsz, N_POOL_GROUPS),
        in_specs=[pl.BlockSpec((1, s, d), lambda b, g: (b, 0, 0)),
                  pl.BlockSpec((d, pg), lambda b, g: (0, g)),
                  pl.BlockSpec((1, pg, pg), lambda b, g: (g, 0, 0)),
                  pl.BlockSpec((1, pg), lambda b, g: (0, g))],
        out_specs=pl.BlockSpec((1, s, pg), lambda b, g: (b, 0, g)),
        out_shape=jax.ShapeDtypeStruct((bsz, s, N_POOL_GROUPS * pg), BF16),
        scratch_shapes=[pltpu.VMEM((s, pg), F32), pltpu.VMEM((s, pg), BF16)],
        compiler_params=_cparams("parallel", "arbitrary"),
        name="pool",
    )(xb, w_pool, wmap, scale)


def _zg_kernel(x_ref, w_ref, o_ref, *, n_silu_tiles):
    j = pl.program_id(1)
    h = _dot(x_ref[...], w_ref[...])
    sg = _sigmoid(h)
    o_ref[...] = (sg * jnp.where(j < n_silu_tiles, h, 1.0)).astype(o_ref.dtype)


def _zg_call(xb2, w_zg, n_silu_cols, tm, tn):
    t, d = xb2.shape
    n = w_zg.shape[1]
    return pl.pallas_call(
        functools.partial(_zg_kernel, n_silu_tiles=n_silu_cols // tn),
        grid=(t // tm, n // tn),
        in_specs=[pl.BlockSpec((tm, d), lambda i, j: (i, 0)),
                  pl.BlockSpec((d, tn), lambda i, j: (0, j))],
        out_specs=pl.BlockSpec((tm, tn), lambda i, j: (i, j)),
        out_shape=jax.ShapeDtypeStruct((t, n), BF16),
        compiler_params=_cparams("parallel", "arbitrary"),
        name="zg",
    )(xb2, w_zg)


def _conv_rows(hp_ref, cw_ref, b_ref, row0, rows, taps):
    pad = taps // 2
    acc = None
    for k in range(taps):
        term = cw_ref[k:k + 1, :] * hp_ref[pl.ds(CONV_PAD_ROWS + row0 - pad + k, rows), :]
        acc = term if acc is None else acc + term
    return acc + b_ref[...]


def _fill_padded(hp_ref, x_ref, w_ref, seq, mm_rows):
    zeros = jnp.zeros((CONV_PAD_ROWS, hp_ref.shape[1]), F32)
    hp_ref[0:CONV_PAD_ROWS, :] = zeros
    hp_ref[CONV_PAD_ROWS + seq:2 * CONV_PAD_ROWS + seq, :] = zeros
    for r in range(seq // mm_rows):
        hp_ref[CONV_PAD_ROWS + r * mm_rows:CONV_PAD_ROWS + (r + 1) * mm_rows, :] = _dot(
            x_ref[0, r * mm_rows:(r + 1) * mm_rows, :], w_ref[...])


def _xbc_kernel(x_ref, w_ref, cw_ref, b_ref, o_ref, hp_ref, *, seq, taps):
    mm_rows = min(512, seq)
    _fill_padded(hp_ref, x_ref, w_ref, seq, mm_rows)
    rows = min(256, seq)
    for r in range(seq // rows):
        v = _conv_rows(hp_ref, cw_ref, b_ref, r * rows, rows, taps)
        o_ref[0, r * rows:(r + 1) * rows, :] = (v * _sigmoid(v)).astype(o_ref.dtype)


def _xbc_call(xb, w_xbc, conv_w, conv_b, tn):
    bsz, s, d = xb.shape
    n = w_xbc.shape[1]
    taps = conv_w.shape[0]
    return pl.pallas_call(
        functools.partial(_xbc_kernel, seq=s, taps=taps),
        grid=(bsz, n // tn),
        in_specs=[pl.BlockSpec((1, s, d), lambda b, j: (b, 0, 0)),
                  pl.BlockSpec((d, tn), lambda b, j: (0, j)),
                  pl.BlockSpec((taps, tn), lambda b, j: (0, j)),
                  pl.BlockSpec((1, tn), lambda b, j: (0, j))],
        out_specs=pl.BlockSpec((1, s, tn), lambda b, j: (b, 0, j)),
        out_shape=jax.ShapeDtypeStruct((bsz, s, n), BF16),
        scratch_shapes=[pltpu.VMEM((s + 2 * CONV_PAD_ROWS, tn), F32)],
        compiler_params=_cparams("parallel", "arbitrary"),
        name="xbc",
    )(xb, w_xbc, conv_w, conv_b)


def _ssd_kernel(x_ref, b_ref, c_ref, zs_ref, rows_ref, acol_ref, colfac_ref, ef_ref, eb_ref,
                dskip_ref, gain_ref, o_ref, y_ref, sf_ref, sb_ref, *, n_chunks):
    L = SSD_CHUNK
    HP = HEAD_GROUP_LANES
    HPG = SSD_HEADS_PER_GROUP
    y_ref[...] = jnp.zeros(y_ref.shape, F32)
    sf_ref[...] = jnp.zeros(sf_ref.shape, F32)
    sb_ref[...] = jnp.zeros(sb_ref.shape, F32)

    li = lax.broadcasted_iota(jnp.int32, (L, L), 0)
    si = lax.broadcasted_iota(jnp.int32, (L, L), 1)
    lower = si < li
    upper = si > li
    lane = lax.broadcasted_iota(jnp.int32, (L, 2 * SSD_HEAD_DIM), 1)
    first_head = lane < SSD_HEAD_DIM

    def step(i, carry):
        cf = i
        cb = n_chunks - 1 - i
        rf = pl.ds(pl.multiple_of(cf * L, L), L)
        rb = pl.ds(pl.multiple_of(cb * L, L), L)

        xc = x_ref[0, rf, :]
        bc = b_ref[0, rf, :]
        cc = c_ref[0, rf, :]
        scores = lax.dot_general(cc, bc, (((1,), (1,)), ((), ())), preferred_element_type=F32)
        arow = rows_ref[0, cf, 0]
        dtrow = rows_ref[0, cf, 1]
        acol = acol_ref[0, 0, rf, :]
        ypairs = []
        for p in range(HPG // 2):
            dmats = []
            for h in (2 * p, 2 * p + 1):
                hb = HPG + h
                seg = jnp.where(lower, acol[:, h:h + 1] - arow[h:h + 1, :],
                                jnp.where(upper, acol[:, hb:hb + 1] - arow[hb:hb + 1, :], 0.0))
                mult = jnp.where(lower, dtrow[h:h + 1, :],
                                 jnp.where(upper, dtrow[hb:hb + 1, :],
                                           dtrow[h:h + 1, :] + dtrow[hb:hb + 1, :]))
                dmats.append((scores * jnp.exp(seg) * mult).astype(BF16))
            dpair = jnp.concatenate(dmats, axis=1)
            xpair = xc[:, p * 2 * SSD_HEAD_DIM:(p + 1) * 2 * SSD_HEAD_DIM]
            zero = jnp.zeros_like(xpair)
            xbd = jnp.concatenate([jnp.where(first_head, xpair, zero),
                                   jnp.where(first_head, zero, xpair)], axis=0)
            ypairs.append(_dot(dpair, xbd))
        ydiag = jnp.concatenate(ypairs, axis=1)

        fac_f = _dot(colfac_ref[0, 0, rf, :], ef_ref[...])
        y_off = _dot(cc, sf_ref[...].astype(BF16)) * fac_f[:, :HP]
        y_ref[rf, :] += ydiag + y_off
        xw = (xc.astype(F32) * fac_f[:, HP:]).astype(BF16)
        upd = lax.dot_general(bc, xw, (((0,), (0,)), ((), ())), preferred_element_type=F32)
        sf_ref[...] = sf_ref[...] * fac_f[L - 1:L, :HP] + upd

        xcb = x_ref[0, rb, :]
        bcb = b_ref[0, rb, :]
        ccb = c_ref[0, rb, :]
        fac_b = _dot(colfac_ref[0, 0, rb, :], eb_ref[...])
        y_ref[rb, :] += _dot(ccb, sb_ref[...].astype(BF16)) * fac_b[:, :HP]
        xwb = (xcb.astype(F32) * fac_b[:, HP:]).astype(BF16)
        updb = lax.dot_general(bcb, xwb, (((0,), (0,)), ((), ())), preferred_element_type=F32)
        sb_ref[...] = sb_ref[...] * fac_b[0:1, :HP] + updb
        return carry

    lax.fori_loop(0, n_chunks, step, 0)

    rows = 256 if (n_chunks * L) % 256 == 0 else L
    for r in range(n_chunks * L // rows):
        sl = slice(r * rows, (r + 1) * rows)
        y = y_ref[sl, :] + dskip_ref[...] * x_ref[0, sl, :].astype(F32)
        v = y * zs_ref[0, sl, :].astype(F32)
        v = v * lax.rsqrt(jnp.mean(v * v, axis=-1, keepdims=True) + RMS_EPS)
        o_ref[0, sl, :] = (v * gain_ref[...]).astype(o_ref.dtype)


def _ssd_call(xbc, zg, rows, acol, colfac, ef, eb, dskip, gain):
    bsz, s, _ = xbc.shape
    nc = s // SSD_CHUNK
    HP = HEAD_GROUP_LANES
    n_x_blocks = SSD_HEADS * SSD_HEAD_DIM // SSD_STATE
    return pl.pallas_call(
        functools.partial(_ssd_kernel, n_chunks=nc),
        grid=(bsz, SSD_GROUPS),
        in_specs=[pl.BlockSpec((1, s, HP), lambda b, g: (b, 0, g)),
                  pl.BlockSpec((1, s, SSD_STATE), lambda b, g: (b, 0, n_x_blocks + g)),
                  pl.BlockSpec((1, s, SSD_STATE), lambda b, g: (b, 0, n_x_blocks + SSD_GROUPS + g)),
                  pl.BlockSpec((1, s, HP), lambda b, g: (b, 0, g)),
                  pl.BlockSpec((1, nc, 2, DIRS_X_HEADS, SSD_CHUNK), lambda b, g: (b, 0, 0, g, 0)),
                  pl.BlockSpec((1, 1, s, LANES), lambda b, g: (b, g, 0, 0)),
                  pl.BlockSpec((1, 1, s, LANES), lambda b, g: (b, g, 0, 0)),
                  pl.BlockSpec((LANES, 2 * HP), lambda b, g: (0, 0)),
                  pl.BlockSpec((LANES, 2 * HP), lambda b, g: (0, 0)),
                  pl.BlockSpec((1, HP), lambda b, g: (0, g)),
                  pl.BlockSpec((1, HP), lambda b, g: (0, g))],
        out_specs=pl.BlockSpec((1, s, HP), lambda b, g: (b, 0, g)),
        out_shape=jax.ShapeDtypeStruct((bsz, s, SSD_GROUPS * HP), BF16),
        scratch_shapes=[pltpu.VMEM((s, HP), F32),
                        pltpu.VMEM((SSD_STATE, HP), F32),
                        pltpu.VMEM((SSD_STATE, HP), F32)],
        compiler_params=_cparams("parallel", "arbitrary"),
        name="ssd",
    )(xbc, xbc, xbc, zg, rows, acol, colfac, ef, eb, dskip, gain)


def _merge_kernel(yn_ref, pool_ref, g0_ref, g1_ref, x_ref, wsp_ref, wout_ref, lng_ref, lnb_ref,
                  of_ref, ob_ref, *, alpha):
    ssd_out = _dot(yn_ref[...], wsp_ref[...])
    merged = g0_ref[...].astype(F32) * pool_ref[...].astype(F32) + g1_ref[...].astype(F32) * ssd_out
    mix = _dot(merged.astype(BF16), wout_ref[...])
    out = _layer_norm(alpha * x_ref[...] + mix, lng_ref[...], lnb_ref[...])
    of_ref[...] = out
    ob_ref[...] = out.astype(BF16)


def _merge_call(yn, pool_out, zg, x, wsp, wout, lng, lnb, alpha, tm):
    t, d = x.shape
    inner = yn.shape[1]
    gate0_block = inner // d
    row = lambda i: (i, 0)
    const = lambda i: (0, 0)
    return pl.pallas_call(
        functools.partial(_merge_kernel, alpha=alpha),
        grid=(t // tm,),
        in_specs=[pl.BlockSpec((tm, inner), row),
                  pl.BlockSpec((tm, d), row),
                  pl.BlockSpec((tm, d), lambda i: (i, gate0_block)),
                  pl.BlockSpec((tm, d), lambda i: (i, gate0_block + 1)),
                  pl.BlockSpec((tm, d), row),
                  pl.BlockSpec((inner, d), const),
                  pl.BlockSpec((d, d), const),
                  pl.BlockSpec((1, d), const),
                  pl.BlockSpec((1, d), const)],
        out_specs=[pl.BlockSpec((tm, d), row), pl.BlockSpec((tm, d), row)],
        out_shape=[jax.ShapeDtypeStruct((t, d), F32), jax.ShapeDtypeStruct((t, d), BF16)],
        compiler_params=_cparams("parallel"),
        name="merge",
    )(yn, pool_out, zg, zg, x, wsp, wout, lng, lnb)


def _ffn_up_kernel(x_ref, wg_ref, wv_ref, cwg_ref, cwv_ref, bg_ref, bv_ref, o_ref, hg_ref, hv_ref,
                   *, seq, taps):
    mm_rows = min(512, seq)
    _fill_padded(hg_ref, x_ref, wg_ref, seq, mm_rows)
    _fill_padded(hv_ref, x_ref, wv_ref, seq, mm_rows)
    rows = min(256, seq)
    for r in range(seq // rows):
        gate = _conv_rows(hg_ref, cwg_ref, bg_ref, r * rows, rows, taps)
        val = _conv_rows(hv_ref, cwv_ref, bv_ref, r * rows, rows, taps)
        act = 0.5 * gate * (1.0 + lax.erf(gate * np.float32(2.0 ** -0.5))) * val
        o_ref[0, r * rows:(r + 1) * rows, :] = act.astype(o_ref.dtype)


def _ffn_up_call(xb, w_up, conv_w, conv_b, tn):
    bsz, s, d = xb.shape
    dff = w_up.shape[1] // 2
    nt = dff // tn
    taps = conv_w.shape[0]
    return pl.pallas_call(
        functools.partial(_ffn_up_kernel, seq=s, taps=taps),
        grid=(bsz, nt),
        in_specs=[pl.BlockSpec((1, s, d), lambda b, j: (b, 0, 0)),
                  pl.BlockSpec((d, tn), lambda b, j: (0, j)),
                  pl.BlockSpec((d, tn), lambda b, j: (0, nt + j)),
                  pl.BlockSpec((taps, tn), lambda b, j: (0, j)),
                  pl.BlockSpec((taps, tn), lambda b, j: (0, nt + j)),
                  pl.BlockSpec((1, tn), lambda b, j: (0, j)),
                  pl.BlockSpec((1, tn), lambda b, j: (0, nt + j))],
        out_specs=pl.BlockSpec((1, s, tn), lambda b, j: (b, 0, j)),
        out_shape=jax.ShapeDtypeStruct((bsz, s, dff), BF16),
        scratch_shapes=[pltpu.VMEM((s + 2 * CONV_PAD_ROWS, tn), F32),
                        pltpu.VMEM((s + 2 * CONV_PAD_ROWS, tn), F32)],
        compiler_params=_cparams("parallel", "arbitrary"),
        name="ffn_up",
    )(xb, w_up, w_up, conv_w, conv_w, conv_b, conv_b)


def _ffn_down_kernel(a_ref, x_ref, w_ref, lng_ref, lnb_ref, of_ref, ob_ref, *, alpha):
    out = _layer_norm(alpha * x_ref[...] + _dot(a_ref[...], w_ref[...]), lng_ref[...], lnb_ref[...])
    of_ref[...] = out
    ob_ref[...] = out.astype(BF16)


def _ffn_down_call(act, x, w_down, lng, lnb, alpha, tm):
    t, d = x.shape
    dff = act.shape[1]
    row = lambda i: (i, 0)
    const = lambda i: (0, 0)
    return pl.pallas_call(
        functools.partial(_ffn_down_kernel, alpha=alpha),
        grid=(t // tm,),
        in_specs=[pl.BlockSpec((tm, dff), row), pl.BlockSpec((tm, d), row),
                  pl.BlockSpec((dff, d), const), pl.BlockSpec((1, d), const),
                  pl.BlockSpec((1, d), const)],
        out_specs=[pl.BlockSpec((tm, d), row), pl.BlockSpec((tm, d), row)],
        out_shape=[jax.ShapeDtypeStruct((t, d), F32), jax.ShapeDtypeStruct((t, d), BF16)],
        compiler_params=_cparams("parallel"),
        name="ffn_down",
    )(act, x, w_down, lng, lnb)


def _placement_constants():
    nh = 2 * SSD_HEADS
    pa = np.zeros((SSD_GROUPS, nh, LANES), np.float32)
    pe = np.zeros((2, SSD_GROUPS, nh, LANES), np.float32)
    pw = np.zeros((2, SSD_GROUPS, nh, LANES), np.float32)
    for g in range(SSD_GROUPS):
        for j in range(DIRS_X_HEADS):
            col = g * DIRS_X_HEADS + j
            pa[g, col, j] = 1.0
            for part in range(2):
                pe[part, g, col, part * 2 * DIRS_X_HEADS + j] = 1.0
                pw[part, g, col, part * 2 * DIRS_X_HEADS + DIRS_X_HEADS + j] = 1.0
    HP = HEAD_GROUP_LANES
    ef = np.zeros((LANES, 2 * HP), np.float32)
    eb = np.zeros((LANES, 2 * HP), np.float32)
    for part in range(2):
        base = part * 2 * DIRS_X_HEADS
        for h in range(SSD_HEADS_PER_GROUP):
            cols = slice(h * SSD_HEAD_DIM, (h + 1) * SSD_HEAD_DIM)
            cols_w = slice(HP + h * SSD_HEAD_DIM, HP + (h + 1) * SSD_HEAD_DIM)
            ef[base + h, cols] = 1.0
            eb[base + SSD_HEADS_PER_GROUP + h, cols] = 1.0
            ef[base + DIRS_X_HEADS + h, cols_w] = 1.0
            eb[base + DIRS_X_HEADS + SSD_HEADS_PER_GROUP + h, cols_w] = 1.0
    as_bf16 = lambda a: jnp.asarray(a, BF16)
    return as_bf16(pa), as_bf16(pe), as_bf16(pw), as_bf16(ef), as_bf16(eb)


def _dt_column_order():
    order = []
    for g in range(SSD_GROUPS):
        for d in range(2):
            for h in range(SSD_HEADS_PER_GROUP):
                order.append(d * SSD_HEADS + g * SSD_HEADS_PER_GROUP + h)
    return np.asarray(order, np.int32)


def kernel(x, w_in, pool_w, pool_scale, ssd_conv_w, ssd_conv_b, a_log, dt_bias, d_skip, ssd_norm_g,
           w_ssd_proj, w_out, ln1_g, ln1_b, w_up, ffn_conv_w, ffn_conv_b, w_down, ln2_g, ln2_b):
    bsz, s, d = x.shape
    depth = w_in.shape[0]
    t = bsz * s
    alpha = float((2 * depth) ** 0.25)
    pool_width = pool_w.shape[1] * pool_w.shape[2]
    inner = w_ssd_proj.shape[1]
    conv_ch = ssd_conv_w.shape[2]
    n_dt = 2 * SSD_HEADS
    c0, c1, c2, c3 = pool_width, pool_width + inner, pool_width + inner + conv_ch, \
        pool_width + inner + conv_ch + n_dt
    tm = min(512, t)
    tm_zg = min(1024, t)
    order = _dt_column_order()
    pa, pe, pw, ef, eb = _placement_constants()

    xf = x.reshape(t, d)
    xb = xf.astype(BF16)
    for i in range(depth):
        wi = w_in[i]
        w_pool = wi[:, :c0].astype(BF16)
        w_zg = jnp.concatenate([wi[:, c0:c1], wi[:, c3:]], axis=1).astype(BF16)
        w_xbc = wi[:, c1:c2].astype(BF16)
        w_dt = wi[:, c2:c3][:, order]
        xb3 = xb.reshape(bsz, s, d)

        dt_b = dt_bias[i].reshape(-1)[order]
        al = a_log[i].reshape(-1)[order]
        rows, acol, colfac = _dt_call(
            xb3, w_dt.astype(BF16), w_dt.T.astype(BF16), dt_b.reshape(1, -1), dt_b.reshape(-1, 1),
            al.reshape(1, -1), al.reshape(-1, 1), pa, pe, pw)
        pool_out = _pool_call(xb3, w_pool, pool_w[i].astype(BF16), pool_scale[i].reshape(1, -1))
        zg = _zg_call(xb, w_zg, inner, tm_zg, 512)
        xbc = _xbc_call(xb3, w_xbc, ssd_conv_w[i], ssd_conv_b[i].reshape(1, -1), 256)
        yn = _ssd_call(xbc, zg.reshape(bsz, s, -1), rows, acol, colfac, ef, eb,
                       jnp.repeat(d_skip[i], SSD_HEAD_DIM).reshape(1, -1),
                       ssd_norm_g[i].reshape(1, -1))
        xf, xb = _merge_call(yn.reshape(t, inner), pool_out.reshape(t, d), zg, xf,
                             w_ssd_proj[i].astype(BF16), w_out[i].astype(BF16),
                             ln1_g[i].reshape(1, -1), ln1_b[i].reshape(1, -1), alpha, tm)
        act = _ffn_up_call(xb.reshape(bsz, s, d), w_up[i].astype(BF16), ffn_conv_w[i],
                           ffn_conv_b[i].reshape(1, -1), 256)
        xf, xb = _ffn_down_call(act.reshape(t, -1), xf, w_down[i].astype(BF16),
                                ln2_g[i].reshape(1, -1), ln2_b[i].reshape(1, -1), alpha, tm)
    return xf.reshape(bsz, s, d)
```

```python
import functools

import numpy as np
import jax
import jax.numpy as jnp
from jax import lax
from jax.experimental import pallas as pl
from jax.experimental.pallas import tpu as pltpu

F32 = jnp.float32
BF16 = jnp.bfloat16

POOL_WINDOWS = (2, 4, 8, 16)
N_POOL_GROUPS = 4
SSD_HEAD_DIM = 64
SSD_GROUPS = 4
SSD_HEADS_PER_GROUP = 8
SSD_HEADS = SSD_GROUPS * SSD_HEADS_PER_GROUP
SSD_STATE = 128
SSD_CHUNK = 128
LN_EPS = 1e-5
RMS_EPS = 1e-5

LANES = 128
SUBLANES = 8
VMEM_LIMIT_BYTES = 56 * 1024 * 1024

CONV_PAD_ROWS = SUBLANES
CONV_PHASES = 4
POOL_BLOCK = 256
POOL_HALO = 16
HEAD_GROUP_LANES = SSD_HEADS_PER_GROUP * SSD_HEAD_DIM
DIRS_X_HEADS = 2 * SSD_HEADS_PER_GROUP


def _cparams(*sem):
    return pltpu.CompilerParams(dimension_semantics=sem, vmem_limit_bytes=VMEM_LIMIT_BYTES)


def _sigmoid(v):
    return 1.0 / (1.0 + jnp.exp(-v))


def _softplus(v):
    return jnp.maximum(v, 0.0) + jnp.log1p(jnp.exp(-jnp.abs(v)))


def _dot(a, b):
    return jnp.dot(a, b, preferred_element_type=F32)


def _split3(v):
    hi = v.astype(BF16)
    r1 = v - hi.astype(F32)
    mid = r1.astype(BF16)
    lo = (r1 - mid.astype(F32)).astype(BF16)
    return hi, mid, lo


def _dot_exact_rhs(m, v):
    hi, mid, lo = _split3(v)
    return _dot(m, hi) + _dot(m, mid) + _dot(m, lo)


def _dot_exact_lhs(v, m):
    hi, mid, lo = _split3(v)
    return _dot(hi, m) + _dot(mid, m) + _dot(lo, m)


def _layer_norm(r, g, b):
    mu = jnp.mean(r, axis=-1, keepdims=True)
    d = r - mu
    var = jnp.mean(d * d, axis=-1, keepdims=True)
    return d * lax.rsqrt(var + LN_EPS) * g + b


def _dt_kernel(x_ref, wdt_ref, wdtT_ref, brow_ref, bcol_ref, alrow_ref, alcol_ref,
               pa_ref, pe_ref, pw_ref, rows_ref, acol_ref, colfac_ref, *, n_chunks):
    L = SSD_CHUNK
    x = x_ref[0]
    dt_c = _softplus(_dot(x, wdt_ref[...]) + brow_ref[...])
    dt_r = _softplus(lax.dot_general(wdtT_ref[...], x, (((1,), (1,)), ((), ())),
                                     preferred_element_type=F32) + bcol_ref[...])
    adt_c = dt_c * (-jnp.exp(alrow_ref[...]))
    adt_r = dt_r * (-jnp.exp(alcol_ref[...]))
    ri = lax.broadcasted_iota(jnp.int32, (L, L), 0)
    ci = lax.broadcasted_iota(jnp.int32, (L, L), 1)
    lower = (ri >= ci).astype(BF16)
    upper = (ri <= ci).astype(BF16)
    nh = dt_c.shape[1]
    isb_row = (lax.broadcasted_iota(jnp.int32, (1, nh), 1) // SSD_HEADS_PER_GROUP) % 2 == 1
    isb_col = (lax.broadcasted_iota(jnp.int32, (nh, 1), 0) // SSD_HEADS_PER_GROUP) % 2 == 1
    for c in range(n_chunks):
        sl = slice(c * L, (c + 1) * L)
        a_c = adt_c[sl]
        cum_c = jnp.where(isb_row, _dot_exact_rhs(upper, a_c), _dot_exact_rhs(lower, a_c))
        tot = jnp.where(isb_row, cum_c[0:1], cum_c[L - 1:L])
        e_a = jnp.exp(cum_c)
        w_in = dt_c[sl] * jnp.exp(tot - cum_c)
        e_hi = e_a.astype(BF16)
        e_lo = (e_a - e_hi.astype(F32)).astype(BF16)
        w_hi = w_in.astype(BF16)
        w_lo = (w_in - w_hi.astype(F32)).astype(BF16)
        for g in range(SSD_GROUPS):
            acol_ref[0, g, sl, :] = _dot_exact_lhs(cum_c, pa_ref[g])
            colfac_ref[0, g, sl, :] = (_dot(e_hi, pe_ref[0, g]) + _dot(w_hi, pw_ref[0, g])
                                       + _dot(e_lo, pe_ref[1, g]) + _dot(w_lo, pw_ref[1, g])
                                       ).astype(BF16)
        a_r = adt_r[:, sl]
        cum_r = jnp.where(isb_col, _dot_exact_lhs(a_r, lower), _dot_exact_lhs(a_r, upper))
        rows_ref[0, c, 0] = cum_r
        rows_ref[0, c, 1] = dt_r[:, sl]


def _dt_call(xb, wdt, wdtT, brow, bcol, alrow, alcol, pa, pe, pw):
    bsz, s, d = xb.shape
    nc = s // SSD_CHUNK
    nh = 2 * SSD_HEADS
    const = lambda *shape: pl.BlockSpec(shape, lambda b: (0,) * len(shape))
    return pl.pallas_call(
        functools.partial(_dt_kernel, n_chunks=nc),
        grid=(bsz,),
        in_specs=[pl.BlockSpec((1, s, d), lambda b: (b, 0, 0)),
                  const(d, nh), const(nh, d), const(1, nh), const(nh, 1), const(1, nh), const(nh, 1),
                  const(SSD_GROUPS, nh, LANES), const(2, SSD_GROUPS, nh, LANES),
                  const(2, SSD_GROUPS, nh, LANES)],
        out_specs=[pl.BlockSpec((1, nc, 2, nh, SSD_CHUNK), lambda b: (b, 0, 0, 0, 0)),
                   pl.BlockSpec((1, SSD_GROUPS, s, LANES), lambda b: (b, 0, 0, 0)),
                   pl.BlockSpec((1, SSD_GROUPS, s, LANES), lambda b: (b, 0, 0, 0))],
        out_shape=[jax.ShapeDtypeStruct((bsz, nc, 2, nh, SSD_CHUNK), F32),
                   jax.ShapeDtypeStruct((bsz, SSD_GROUPS, s, LANES), F32),
                   jax.ShapeDtypeStruct((bsz, SSD_GROUPS, s, LANES), BF16)],
        compiler_params=_cparams("parallel"),
        name="dt",
    )(xb, wdt, wdtT, brow, bcol, alrow, alcol, pa, pe, pw)


def _pool_kernel(x_ref, w_ref, wmap_ref, scale_ref, o_ref, u_ref, ub_ref, *, seq):
    g = pl.program_id(1)
    half = jnp.left_shift(1, g)
    u = _dot(x_ref[0], w_ref[...])
    u_ref[...] = u
    ub_ref[...] = u.astype(BF16)
    blk = min(POOL_BLOCK, seq)
    win = min(seq, blk + 2 * POOL_HALO)
    for r in range(seq // blk):
        start = max(0, min(r * blk - POOL_HALO, seq - win))
        t = r * blk + lax.broadcasted_iota(jnp.int32, (blk, win), 0)
        j = start + lax.broadcasted_iota(jnp.int32, (blk, win), 1)
        band = ((j >= t - half) & (j <= t + half - 1)).astype(BF16)
        ssum = _dot(band, ub_ref[start:start + win, :])
        tq = r * blk + lax.broadcasted_iota(jnp.int32, ssum.shape, 0)
        cnt = jnp.minimum(tq + half - 1, seq - 1) - jnp.maximum(tq - half, 0) + 1
        pooled = ssum / cnt.astype(F32) - u_ref[r * blk:(r + 1) * blk, :]
        out = _dot(pooled.astype(BF16), wmap_ref[0]) * scale_ref[...]
        o_ref[0, r * blk:(r + 1) * blk, :] = out.astype(o_ref.dtype)


def _pool_call(xb, w_pool, wmap, scale):
    bsz, s, d = xb.shape
    pg = w_pool.shape[1] // N_POOL_GROUPS
    return pl.pallas_call(
        functools.partial(_pool_kernel, seq=s),
        grid=(bsz, N_POOL_GROUPS),
        in_specs=[pl.BlockSpec((1, s, d), lambda b, g: (b, 0, 0)),
                  pl.BlockSpec((d, pg), lambda b, g: (0, g)),
                  pl.BlockSpec((1, pg, pg), lambda b, g: (g, 0, 0)),
                  pl.BlockSpec((1, pg), lambda b, g: (0, g))],
        out_specs=pl.BlockSpec((1, s, pg), lambda b, g: (b, 0, g)),
        out_shape=jax.ShapeDtypeStruct((bsz, s, N_POOL_GROUPS * pg), BF16),
        scratch_shapes=[pltpu.VMEM((s, pg), F32), pltpu.VMEM((s, pg), BF16)],
        compiler_params=_cparams("parallel", "arbitrary"),
        name="pool",
    )(xb, w_pool, wmap, scale)


def _zg_kernel(x_ref, w_ref, o_ref, *, n_silu_tiles):
    j = pl.program_id(1)
    h = _dot(x_ref[...], w_ref[...])
    sg = _sigmoid(h)
    o_ref[...] = (sg * jnp.where(j < n_silu_tiles, h, 1.0)).astype(o_ref.dtype)


def _zg_call(xb2, w_zg, n_silu_cols, tm, tn):
    t, d = xb2.shape
    n = w_zg.shape[1]
    return pl.pallas_call(
        functools.partial(_zg_kernel, n_silu_tiles=n_silu_cols // tn),
        grid=(t // tm, n // tn),
        in_specs=[pl.BlockSpec((tm, d), lambda i, j: (i, 0)),
                  pl.BlockSpec((d, tn), lambda i, j: (0, j))],
        out_specs=pl.BlockSpec((tm, tn), lambda i, j: (i, j)),
        out_shape=jax.ShapeDtypeStruct((t, n), BF16),
        compiler_params=_cparams("parallel", "arbitrary"),
        name="zg",
    )(xb2, w_zg)


def _fill_padded(hp_ref, x_ref, w_ref, seq, mm_rows):
    zeros = jnp.zeros((CONV_PAD_ROWS, LANES), F32)
    for sl in range(hp_ref.shape[0]):
        hp_ref[sl, 0:CONV_PAD_ROWS, :] = zeros
        hp_ref[sl, CONV_PAD_ROWS + seq:2 * CONV_PAD_ROWS + seq, :] = zeros
    for r in range(seq // mm_rows):
        res = _dot(x_ref[0, r * mm_rows:(r + 1) * mm_rows, :], w_ref[...])
        for sl in range(hp_ref.shape[0]):
            hp_ref[sl, CONV_PAD_ROWS + r * mm_rows:CONV_PAD_ROWS + (r + 1) * mm_rows, :] = (
                res[:, sl * LANES:(sl + 1) * LANES])


def _conv_phases(hp_ref, sl, cw_ref, b_ref, row0, rows, taps):
    pad = taps // 2
    n = rows // CONV_PHASES
    lanes = slice(sl * LANES, (sl + 1) * LANES)
    shifted = {m: hp_ref[sl, pl.ds(CONV_PAD_ROWS + row0 + m, n, stride=CONV_PHASES), :]
               for m in range(-pad, CONV_PHASES + pad)}
    outs = []
    for j in range(CONV_PHASES):
        acc = b_ref[:, lanes]
        for k in range(taps):
            acc = acc + cw_ref[k:k + 1, lanes] * shifted[j + k - pad]
        outs.append(acc)
    return outs


def _store_phases(st_ref, sl, row0, rows, phases):
    for j, v in enumerate(phases):
        st_ref[sl, pl.ds(row0 + j, rows // CONV_PHASES, stride=CONV_PHASES), :] = v


def _xbc_kernel(x_ref, w_ref, cw_ref, b_ref, o_ref, hp_ref, st_ref, *, seq, taps):
    mm_rows = min(512, seq)
    _fill_padded(hp_ref, x_ref, w_ref, seq, mm_rows)
    rows = min(256, seq)
    for r in range(seq // rows):
        for sl in range(hp_ref.shape[0]):
            phases = _conv_phases(hp_ref, sl, cw_ref, b_ref, r * rows, rows, taps)
            _store_phases(st_ref, sl, r * rows, rows, [v * _sigmoid(v) for v in phases])
            o_ref[0, r * rows:(r + 1) * rows, sl * LANES:(sl + 1) * LANES] = (
                st_ref[sl, r * rows:(r + 1) * rows, :].astype(o_ref.dtype))


def _xbc_call(xb, w_xbc, conv_w, conv_b, tn):
    bsz, s, d = xb.shape
    n = w_xbc.shape[1]
    taps = conv_w.shape[0]
    return pl.pallas_call(
        functools.partial(_xbc_kernel, seq=s, taps=taps),
        grid=(bsz, n // tn),
        in_specs=[pl.BlockSpec((1, s, d), lambda b, j: (b, 0, 0)),
                  pl.BlockSpec((d, tn), lambda b, j: (0, j)),
                  pl.BlockSpec((taps, tn), lambda b, j: (0, j)),
                  pl.BlockSpec((1, tn), lambda b, j: (0, j))],
        out_specs=pl.BlockSpec((1, s, tn), lambda b, j: (b, 0, j)),
        out_shape=jax.ShapeDtypeStruct((bsz, s, n), BF16),
        scratch_shapes=[pltpu.VMEM((tn // LANES, s + 2 * CONV_PAD_ROWS, LANES), F32),
                        pltpu.VMEM((tn // LANES, s, LANES), F32)],
        compiler_params=_cparams("parallel", "arbitrary"),
        name="xbc",
    )(xb, w_xbc, conv_w, conv_b)


def _ssd_kernel(x_ref, b_ref, c_ref, zs_ref, rows_ref, acol_ref, colfac_ref, ef_ref, eb_ref,
                dskip_ref, gain_ref, o_ref, y_ref, sf_ref, sb_ref, *, n_chunks):
    L = SSD_CHUNK
    HP = HEAD_GROUP_LANES
    HPG = SSD_HEADS_PER_GROUP
    y_ref[...] = jnp.zeros(y_ref.shape, F32)
    sf_ref[...] = jnp.zeros(sf_ref.shape, F32)
    sb_ref[...] = jnp.zeros(sb_ref.shape, F32)

    li = lax.broadcasted_iota(jnp.int32, (L, L), 0)
    si = lax.broadcasted_iota(jnp.int32, (L, L), 1)
    lower = si < li
    upper = si > li
    lane = lax.broadcasted_iota(jnp.int32, (L, 2 * SSD_HEAD_DIM), 1)
    first_head = lane < SSD_HEAD_DIM

    def step(i, carry):
        cf = i
        cb = n_chunks - 1 - i
        rf = pl.ds(pl.multiple_of(cf * L, L), L)
        rb = pl.ds(pl.multiple_of(cb * L, L), L)

        xc = x_ref[0, rf, :]
        bc = b_ref[0, rf, :]
        cc = c_ref[0, rf, :]
        scores = lax.dot_general(cc, bc, (((1,), (1,)), ((), ())), preferred_element_type=F32)
        arow = rows_ref[0, cf, 0]
        dtrow = rows_ref[0, cf, 1]
        acol = acol_ref[0, 0, rf, :]
        ypairs = []
        for p in range(HPG // 2):
            dmats = []
            for h in (2 * p, 2 * p + 1):
                hb = HPG + h
                seg = jnp.where(lower, acol[:, h:h + 1] - arow[h:h + 1, :],
                                jnp.where(upper, acol[:, hb:hb + 1] - arow[hb:hb + 1, :], 0.0))
                mult = jnp.where(lower, dtrow[h:h + 1, :],
                                 jnp.where(upper, dtrow[hb:hb + 1, :],
                                           dtrow[h:h + 1, :] + dtrow[hb:hb + 1, :]))
                dmats.append((scores * jnp.exp(seg) * mult).astype(BF16))
            dpair = jnp.concatenate(dmats, axis=1)
            xpair = xc[:, p * 2 * SSD_HEAD_DIM:(p + 1) * 2 * SSD_HEAD_DIM]
            zero = jnp.zeros_like(xpair)
            xbd = jnp.concatenate([jnp.where(first_head, xpair, zero),
                                   jnp.where(first_head, zero, xpair)], axis=0)
            ypairs.append(_dot(dpair, xbd))
        ydiag = jnp.concatenate(ypairs, axis=1)

        fac_f = _dot(colfac_ref[0, 0, rf, :], ef_ref[...])
        y_off = _dot(cc, sf_ref[...].astype(BF16)) * fac_f[:, :HP]
        y_ref[rf, :] += ydiag + y_off
        xw = (xc.astype(F32) * fac_f[:, HP:]).astype(BF16)
        upd = lax.dot_general(bc, xw, (((0,), (0,)), ((), ())), preferred_element_type=F32)
        sf_ref[...] = sf_ref[...] * fac_f[L - 1:L, :HP] + upd

        xcb = x_ref[0, rb, :]
        bcb = b_ref[0, rb, :]
        ccb = c_ref[0, rb, :]
        fac_b = _dot(colfac_ref[0, 0, rb, :], eb_ref[...])
        y_ref[rb, :] += _dot(ccb, sb_ref[...].astype(BF16)) * fac_b[:, :HP]
        xwb = (xcb.astype(F32) * fac_b[:, HP:]).astype(BF16)
        updb = lax.dot_general(bcb, xwb, (((0,), (0,)), ((), ())), preferred_element_type=F32)
        sb_ref[...] = sb_ref[...] * fac_b[0:1, :HP] + updb
        return carry

    lax.fori_loop(0, n_chunks, step, 0)

    rows = 256 if (n_chunks * L) % 256 == 0 else L
    for r in range(n_chunks * L // rows):
        sl = slice(r * rows, (r + 1) * rows)
        y = y_ref[sl, :] + dskip_ref[...] * x_ref[0, sl, :].astype(F32)
        v = y * zs_ref[0, sl, :].astype(F32)
        v = v * lax.rsqrt(jnp.mean(v * v, axis=-1, keepdims=True) + RMS_EPS)
        o_ref[0, sl, :] = (v * gain_ref[...]).astype(o_ref.dtype)


def _ssd_call(xbc, zg, rows, acol, colfac, ef, eb, dskip, gain):
    bsz, s, _ = xbc.shape
    nc = s // SSD_CHUNK
    HP = HEAD_GROUP_LANES
    n_x_blocks = SSD_HEADS * SSD_HEAD_DIM // SSD_STATE
    return pl.pallas_call(
        functools.partial(_ssd_kernel, n_chunks=nc),
        grid=(bsz, SSD_GROUPS),
        in_specs=[pl.BlockSpec((1, s, HP), lambda b, g: (b, 0, g)),
                  pl.BlockSpec((1, s, SSD_STATE), lambda b, g: (b, 0, n_x_blocks + g)),
                  pl.BlockSpec((1, s, SSD_STATE), lambda b, g: (b, 0, n_x_blocks + SSD_GROUPS + g)),
                  pl.BlockSpec((1, s, HP), lambda b, g: (b, 0, g)),
                  pl.BlockSpec((1, nc, 2, DIRS_X_HEADS, SSD_CHUNK), lambda b, g: (b, 0, 0, g, 0)),
                  pl.BlockSpec((1, 1, s, LANES), lambda b, g: (b, g, 0, 0)),
                  pl.BlockSpec((1, 1, s, LANES), lambda b, g: (b, g, 0, 0)),
                  pl.BlockSpec((LANES, 2 * HP), lambda b, g: (0, 0)),
                  pl.BlockSpec((LANES, 2 * HP), lambda b, g: (0, 0)),
                  pl.BlockSpec((1, HP), lambda b, g: (0, g)),
                  pl.BlockSpec((1, HP), lambda b, g: (0, g))],
        out_specs=pl.BlockSpec((1, s, HP), lambda b, g: (b, 0, g)),
        out_shape=jax.ShapeDtypeStruct((bsz, s, SSD_GROUPS * HP), BF16),
        scratch_shapes=[pltpu.VMEM((s, HP), F32),
                        pltpu.VMEM((SSD_STATE, HP), F32),
                        pltpu.VMEM((SSD_STATE, HP), F32)],
        compiler_params=_cparams("parallel", "arbitrary"),
        name="ssd",
    )(xbc, xbc, xbc, zg, rows, acol, colfac, ef, eb, dskip, gain)


def _merge_kernel(yn_ref, pool_ref, g0_ref, g1_ref, x_ref, wsp_ref, wout_ref, lng_ref, lnb_ref,
                  of_ref, ob_ref, *, alpha):
    ssd_out = _dot(yn_ref[...], wsp_ref[...])
    merged = g0_ref[...].astype(F32) * pool_ref[...].astype(F32) + g1_ref[...].astype(F32) * ssd_out
    mix = _dot(merged.astype(BF16), wout_ref[...])
    out = _layer_norm(alpha * x_ref[...] + mix, lng_ref[...], lnb_ref[...])
    of_ref[...] = out
    ob_ref[...] = out.astype(BF16)


def _merge_call(yn, pool_out, zg, x, wsp, wout, lng, lnb, alpha, tm):
    t, d = x.shape
    inner = yn.shape[1]
    gate0_block = inner // d
    row = lambda i: (i, 0)
    const = lambda i: (0, 0)
    return pl.pallas_call(
        functools.partial(_merge_kernel, alpha=alpha),
        grid=(t // tm,),
        in_specs=[pl.BlockSpec((tm, inner), row),
                  pl.BlockSpec((tm, d), row),
                  pl.BlockSpec((tm, d), lambda i: (i, gate0_block)),
                  pl.BlockSpec((tm, d), lambda i: (i, gate0_block + 1)),
                  pl.BlockSpec((tm, d), row),
                  pl.BlockSpec((inner, d), const),
                  pl.BlockSpec((d, d), const),
                  pl.BlockSpec((1, d), const),
                  pl.BlockSpec((1, d), const)],
        out_specs=[pl.BlockSpec((tm, d), row), pl.BlockSpec((tm, d), row)],
        out_shape=[jax.ShapeDtypeStruct((t, d), F32), jax.ShapeDtypeStruct((t, d), BF16)],
        compiler_params=_cparams("parallel"),
        name="merge",
    )(yn, pool_out, zg, zg, x, wsp, wout, lng, lnb)


def _ffn_up_kernel(x_ref, wg_ref, wv_ref, cwg_ref, cwv_ref, bg_ref, bv_ref, o_ref, hg_ref, hv_ref,
                   st_ref, *, seq, taps):
    mm_rows = min(512, seq)
    _fill_padded(hg_ref, x_ref, wg_ref, seq, mm_rows)
    _fill_padded(hv_ref, x_ref, wv_ref, seq, mm_rows)
    rows = min(256, seq)
    for r in range(seq // rows):
        for sl in range(hg_ref.shape[0]):
            gates = _conv_phases(hg_ref, sl, cwg_ref, bg_ref, r * rows, rows, taps)
            vals = _conv_phases(hv_ref, sl, cwv_ref, bv_ref, r * rows, rows, taps)
            acts = [0.5 * g * (1.0 + lax.erf(g * np.float32(2.0 ** -0.5))) * v
                    for g, v in zip(gates, vals)]
            _store_phases(st_ref, sl, r * rows, rows, acts)
            o_ref[0, r * rows:(r + 1) * rows, sl * LANES:(sl + 1) * LANES] = (
                st_ref[sl, r * rows:(r + 1) * rows, :].astype(o_ref.dtype))


def _ffn_up_call(xb, w_up, conv_w, conv_b, tn):
    bsz, s, d = xb.shape
    dff = w_up.shape[1] // 2
    nt = dff // tn
    taps = conv_w.shape[0]
    return pl.pallas_call(
        functools.partial(_ffn_up_kernel, seq=s, taps=taps),
        grid=(bsz, nt),
        in_specs=[pl.BlockSpec((1, s, d), lambda b, j: (b, 0, 0)),
                  pl.BlockSpec((d, tn), lambda b, j: (0, j)),
                  pl.BlockSpec((d, tn), lambda b, j: (0, nt + j)),
                  pl.BlockSpec((taps, tn), lambda b, j: (0, j)),
                  pl.BlockSpec((taps, tn), lambda b, j: (0, nt + j)),
                  pl.BlockSpec((1, tn), lambda b, j: (0, j)),
                  pl.BlockSpec((1, tn), lambda b, j: (0, nt + j))],
        out_specs=pl.BlockSpec((1, s, tn), lambda b, j: (b, 0, j)),
        out_shape=jax.ShapeDtypeStruct((bsz, s, dff), BF16),
        scratch_shapes=[pltpu.VMEM((tn // LANES, s + 2 * CONV_PAD_ROWS, LANES), F32),
                        pltpu.VMEM((tn // LANES, s + 2 * CONV_PAD_ROWS, LANES), F32),
                        pltpu.VMEM((tn // LANES, s, LANES), F32)],
        compiler_params=_cparams("parallel", "arbitrary"),
        name="ffn_up",
    )(xb, w_up, w_up, conv_w, conv_w, conv_b, conv_b)


def _ffn_down_kernel(a_ref, x_ref, w_ref, lng_ref, lnb_ref, of_ref, ob_ref, *, alpha):
    out = _layer_norm(alpha * x_ref[...] + _dot(a_ref[...], w_ref[...]), lng_ref[...], lnb_ref[...])
    of_ref[...] = out
    ob_ref[...] = out.astype(BF16)


def _ffn_down_call(act, x, w_down, lng, lnb, alpha, tm):
    t, d = x.shape
    dff = act.shape[1]
    row = lambda i: (i, 0)
    const = lambda i: (0, 0)
    return pl.pallas_call(
        functools.partial(_ffn_down_kernel, alpha=alpha),
        grid=(t // tm,),
        in_specs=[pl.BlockSpec((tm, dff), row), pl.BlockSpec((tm, d), row),
                  pl.BlockSpec((dff, d), const), pl.BlockSpec((1, d), const),
                  pl.BlockSpec((1, d), const)],
        out_specs=[pl.BlockSpec((tm, d), row), pl.BlockSpec((tm, d), row)],
        out_shape=[jax.ShapeDtypeStruct((t, d), F32), jax.ShapeDtypeStruct((t, d), BF16)],
        compiler_params=_cparams("parallel"),
        name="ffn_down",
    )(act, x, w_down, lng, lnb)


def _placement_constants():
    nh = 2 * SSD_HEADS
    pa = np.zeros((SSD_GROUPS, nh, LANES), np.float32)
    pe = np.zeros((2, SSD_GROUPS, nh, LANES), np.float32)
    pw = np.zeros((2, SSD_GROUPS, nh, LANES), np.float32)
    for g in range(SSD_GROUPS):
        for j in range(DIRS_X_HEADS):
            col = g * DIRS_X_HEADS + j
            pa[g, col, j] = 1.0
            for part in range(2):
                pe[part, g, col, part * 2 * DIRS_X_HEADS + j] = 1.0
                pw[part, g, col, part * 2 * DIRS_X_HEADS + DIRS_X_HEADS + j] = 1.0
    HP = HEAD_GROUP_LANES
    ef = np.zeros((LANES, 2 * HP), np.float32)
    eb = np.zeros((LANES, 2 * HP), np.float32)
    for part in range(2):
        base = part * 2 * DIRS_X_HEADS
        for h in range(SSD_HEADS_PER_GROUP):
            cols = slice(h * SSD_HEAD_DIM, (h + 1) * SSD_HEAD_DIM)
            cols_w = slice(HP + h * SSD_HEAD_DIM, HP + (h + 1) * SSD_HEAD_DIM)
            ef[base + h, cols] = 1.0
            eb[base + SSD_HEADS_PER_GROUP + h, cols] = 1.0
            ef[base + DIRS_X_HEADS + h, cols_w] = 1.0
            eb[base + DIRS_X_HEADS + SSD_HEADS_PER_GROUP + h, cols_w] = 1.0
    as_bf16 = lambda a: jnp.asarray(a, BF16)
    return as_bf16(pa), as_bf16(pe), as_bf16(pw), as_bf16(ef), as_bf16(eb)


def _dt_column_order():
    order = []
    for g in range(SSD_GROUPS):
        for d in range(2):
            for h in range(SSD_HEADS_PER_GROUP):
                order.append(d * SSD_HEADS + g * SSD_HEADS_PER_GROUP + h)
    return np.asarray(order, np.int32)


def kernel(x, w_in, pool_w, pool_scale, ssd_conv_w, ssd_conv_b, a_log, dt_bias, d_skip, ssd_norm_g,
           w_ssd_proj, w_out, ln1_g, ln1_b, w_up, ffn_conv_w, ffn_conv_b, w_down, ln2_g, ln2_b):
    bsz, s, d = x.shape
    depth = w_in.shape[0]
    t = bsz * s
    alpha = float((2 * depth) ** 0.25)
    pool_width = pool_w.shape[1] * pool_w.shape[2]
    inner = w_ssd_proj.shape[1]
    conv_ch = ssd_conv_w.shape[2]
    n_dt = 2 * SSD_HEADS
    c0, c1, c2, c3 = pool_width, pool_width + inner, pool_width + inner + conv_ch, \
        pool_width + inner + conv_ch + n_dt
    tm = min(512, t)
    tm_zg = min(1024, t)
    order = _dt_column_order()
    pa, pe, pw, ef, eb = _placement_constants()

    xf = x.reshape(t, d)
    xb = xf.astype(BF16)
    for i in range(depth):
        wi = w_in[i]
        w_pool = wi[:, :c0].astype(BF16)
        w_zg = jnp.concatenate([wi[:, c0:c1], wi[:, c3:]], axis=1).astype(BF16)
        w_xbc = wi[:, c1:c2].astype(BF16)
        w_dt = wi[:, c2:c3][:, order]
        xb3 = xb.reshape(bsz, s, d)

        dt_b = dt_bias[i].reshape(-1)[order]
        al = a_log[i].reshape(-1)[order]
        rows, acol, colfac = _dt_call(
            xb3, w_dt.astype(BF16), w_dt.T.astype(BF16), dt_b.reshape(1, -1), dt_b.reshape(-1, 1),
            al.reshape(1, -1), al.reshape(-1, 1), pa, pe, pw)
        pool_out = _pool_call(xb3, w_pool, pool_w[i].astype(BF16), pool_scale[i].reshape(1, -1))
        zg = _zg_call(xb, w_zg, inner, tm_zg, 512)
        xbc = _xbc_call(xb3, w_xbc, ssd_conv_w[i], ssd_conv_b[i].reshape(1, -1), 256)
        yn = _ssd_call(xbc, zg.reshape(bsz, s, -1), rows, acol, colfac, ef, eb,
                       jnp.repeat(d_skip[i], SSD_HEAD_DIM).reshape(1, -1),
                       ssd_norm_g[i].reshape(1, -1))
        xf, xb = _merge_call(yn.reshape(t, inner), pool_out.reshape(t, d), zg, xf,
                             w_ssd_proj[i].astype(BF16), w_out[i].astype(BF16),
                             ln1_g[i].reshape(1, -1), ln1_b[i].reshape(1, -1), alpha, tm)
        act = _ffn_up_call(xb.reshape(bsz, s, d), w_up[i].astype(BF16), ffn_conv_w[i],
                           ffn_conv_b[i].reshape(1, -1), 256)
        xf, xb = _ffn_down_call(act.reshape(t, -1), xf, w_down[i].astype(BF16),
                                ln2_g[i].reshape(1, -1), ln2_b[i].reshape(1, -1), alpha, tm)
    return xf.reshape(bsz, s, d)
```

```python
import functools

import numpy as np
import jax
import jax.numpy as jnp
from jax import lax
from jax.experimental import pallas as pl
from jax.experimental.pallas import tpu as pltpu

F32 = jnp.float32
BF16 = jnp.bfloat16

POOL_WINDOWS = (2, 4, 8, 16)
N_POOL_GROUPS = 4
SSD_HEAD_DIM = 64
SSD_GROUPS = 4
SSD_HEADS_PER_GROUP = 8
SSD_HEADS = SSD_GROUPS * SSD_HEADS_PER_GROUP
SSD_STATE = 128
SSD_CHUNK = 128
LN_EPS = 1e-5
RMS_EPS = 1e-5

LANES = 128
SUBLANES = 8
VMEM_LIMIT_BYTES = 56 * 1024 * 1024

CONV_PAD_ROWS = SUBLANES
CONV_PHASES = 4
POOL_BLOCK = 256
POOL_HALO = 16
HEAD_GROUP_LANES = SSD_HEADS_PER_GROUP * SSD_HEAD_DIM
DIRS_X_HEADS = 2 * SSD_HEADS_PER_GROUP
LOCAL_UNROLL = 2


def _cparams(*sem):
    return pltpu.CompilerParams(dimension_semantics=sem, vmem_limit_bytes=VMEM_LIMIT_BYTES)


def _sigmoid(v):
    return 1.0 / (1.0 + jnp.exp(-v))


def _softplus(v):
    return jnp.maximum(v, 0.0) + jnp.log1p(jnp.exp(-jnp.abs(v)))


def _dot(a, b):
    return jnp.dot(a, b, preferred_element_type=F32)


def _split3(v):
    hi = v.astype(BF16)
    r1 = v - hi.astype(F32)
    mid = r1.astype(BF16)
    lo = (r1 - mid.astype(F32)).astype(BF16)
    return hi, mid, lo


def _dot_exact_rhs(m, v):
    hi, mid, lo = _split3(v)
    return _dot(m, hi) + _dot(m, mid) + _dot(m, lo)


def _dot_exact_lhs(v, m):
    hi, mid, lo = _split3(v)
    return _dot(hi, m) + _dot(mid, m) + _dot(lo, m)


def _layer_norm(r, g, b):
    mu = jnp.mean(r, axis=-1, keepdims=True)
    d = r - mu
    var = jnp.mean(d * d, axis=-1, keepdims=True)
    return d * lax.rsqrt(var + LN_EPS) * g + b


def _dt_kernel(x_ref, wdt_ref, wdtT_ref, brow_ref, bcol_ref, alrow_ref, alcol_ref,
               pa_ref, pe_ref, pw_ref, rows_ref, acol_ref, colfac_ref, *, n_chunks):
    L = SSD_CHUNK
    x = x_ref[0]
    dt_c = _softplus(_dot(x, wdt_ref[...]) + brow_ref[...])
    dt_r = _softplus(lax.dot_general(wdtT_ref[...], x, (((1,), (1,)), ((), ())),
                                     preferred_element_type=F32) + bcol_ref[...])
    adt_c = dt_c * (-jnp.exp(alrow_ref[...]))
    adt_r = dt_r * (-jnp.exp(alcol_ref[...]))
    log_dt_r = jnp.log(dt_r)
    hpg = SSD_HEADS_PER_GROUP
    pieces = []
    for g in range(SSD_GROUPS):
        both = jnp.log(dt_r[2 * g * hpg:(2 * g + 1) * hpg] + dt_r[(2 * g + 1) * hpg:(2 * g + 2) * hpg])
        pieces += [both, both]
    log_dt_both = jnp.concatenate(pieces, axis=0)
    ri = lax.broadcasted_iota(jnp.int32, (L, L), 0)
    ci = lax.broadcasted_iota(jnp.int32, (L, L), 1)
    lower = (ri >= ci).astype(BF16)
    upper = (ri <= ci).astype(BF16)
    nh = dt_c.shape[1]
    isb_row = (lax.broadcasted_iota(jnp.int32, (1, nh), 1) // SSD_HEADS_PER_GROUP) % 2 == 1
    isb_col = (lax.broadcasted_iota(jnp.int32, (nh, 1), 0) // SSD_HEADS_PER_GROUP) % 2 == 1
    for c in range(n_chunks):
        sl = slice(c * L, (c + 1) * L)
        a_c = adt_c[sl]
        cum_c = jnp.where(isb_row, _dot_exact_rhs(upper, a_c), _dot_exact_rhs(lower, a_c))
        tot = jnp.where(isb_row, cum_c[0:1], cum_c[L - 1:L])
        e_a = jnp.exp(cum_c)
        w_in = dt_c[sl] * jnp.exp(tot - cum_c)
        e_hi = e_a.astype(BF16)
        e_lo = (e_a - e_hi.astype(F32)).astype(BF16)
        w_hi = w_in.astype(BF16)
        w_lo = (w_in - w_hi.astype(F32)).astype(BF16)
        for g in range(SSD_GROUPS):
            acol_ref[0, g, sl, :] = _dot_exact_lhs(cum_c, pa_ref[g])
            colfac_ref[0, g, sl, :] = (_dot(e_hi, pe_ref[0, g]) + _dot(w_hi, pw_ref[0, g])
                                       + _dot(e_lo, pe_ref[1, g]) + _dot(w_lo, pw_ref[1, g])
                                       ).astype(BF16)
        a_r = adt_r[:, sl]
        cum_r = jnp.where(isb_col, _dot_exact_lhs(a_r, lower), _dot_exact_lhs(a_r, upper))
        rows_ref[0, c, 0] = cum_r - log_dt_r[:, sl]
        rows_ref[0, c, 1] = log_dt_both[:, sl]


def _dt_call(xb, wdt, wdtT, brow, bcol, alrow, alcol, pa, pe, pw):
    bsz, s, d = xb.shape
    nc = s // SSD_CHUNK
    nh = 2 * SSD_HEADS
    const = lambda *shape: pl.BlockSpec(shape, lambda b: (0,) * len(shape))
    return pl.pallas_call(
        functools.partial(_dt_kernel, n_chunks=nc),
        grid=(bsz,),
        in_specs=[pl.BlockSpec((1, s, d), lambda b: (b, 0, 0)),
                  const(d, nh), const(nh, d), const(1, nh), const(nh, 1), const(1, nh), const(nh, 1),
                  const(SSD_GROUPS, nh, LANES), const(2, SSD_GROUPS, nh, LANES),
                  const(2, SSD_GROUPS, nh, LANES)],
        out_specs=[pl.BlockSpec((1, nc, 2, nh, SSD_CHUNK), lambda b: (b, 0, 0, 0, 0)),
                   pl.BlockSpec((1, SSD_GROUPS, s, LANES), lambda b: (b, 0, 0, 0)),
                   pl.BlockSpec((1, SSD_GROUPS, s, LANES), lambda b: (b, 0, 0, 0))],
        out_shape=[jax.ShapeDtypeStruct((bsz, nc, 2, nh, SSD_CHUNK), F32),
                   jax.ShapeDtypeStruct((bsz, SSD_GROUPS, s, LANES), F32),
                   jax.ShapeDtypeStruct((bsz, SSD_GROUPS, s, LANES), BF16)],
        compiler_params=_cparams("parallel"),
        name="dt",
    )(xb, wdt, wdtT, brow, bcol, alrow, alcol, pa, pe, pw)


def _pool_kernel(x_ref, w_ref, wmap_ref, scale_ref, o_ref, u_ref, ub_ref, *, seq):
    g = pl.program_id(1)
    half = jnp.left_shift(1, g)
    u = _dot(x_ref[0], w_ref[0].astype(BF16))
    u_ref[...] = u
    ub_ref[...] = u.astype(BF16)
    blk = min(POOL_BLOCK, seq)
    win = min(seq, blk + 2 * POOL_HALO)
    for r in range(seq // blk):
        start = max(0, min(r * blk - POOL_HALO, seq - win))
        t = r * blk + lax.broadcasted_iota(jnp.int32, (blk, win), 0)
        j = start + lax.broadcasted_iota(jnp.int32, (blk, win), 1)
        band = ((j >= t - half) & (j <= t + half - 1)).astype(BF16)
        ssum = _dot(band, ub_ref[start:start + win, :])
        tq = r * blk + lax.broadcasted_iota(jnp.int32, ssum.shape, 0)
        cnt = jnp.minimum(tq + half - 1, seq - 1) - jnp.maximum(tq - half, 0) + 1
        pooled = ssum / cnt.astype(F32) - u_ref[r * blk:(r + 1) * blk, :]
        out = _dot(pooled.astype(BF16), wmap_ref[0, 0].astype(BF16)) * scale_ref[...]
        o_ref[0, r * blk:(r + 1) * blk, :] = out.astype(o_ref.dtype)


def _pool_call(xb, w_in, layer, wmap, scale):
    bsz, s, d = xb.shape
    pg = wmap.shape[-1]
    return pl.pallas_call(
        functools.partial(_pool_kernel, seq=s),
        grid=(bsz, N_POOL_GROUPS),
        in_specs=[pl.BlockSpec((1, s, d), lambda b, g: (b, 0, 0)),
                  pl.BlockSpec((1, d, pg), lambda b, g: (layer, 0, g)),
                  pl.BlockSpec((1, 1, pg, pg), lambda b, g: (layer, g, 0, 0)),
                  pl.BlockSpec((1, pg), lambda b, g: (0, g))],
        out_specs=pl.BlockSpec((1, s, pg), lambda b, g: (b, 0, g)),
        out_shape=jax.ShapeDtypeStruct((bsz, s, N_POOL_GROUPS * pg), BF16),
        scratch_shapes=[pltpu.VMEM((s, pg), F32), pltpu.VMEM((s, pg), BF16)],
        compiler_params=_cparams("parallel", "arbitrary"),
        name="pool",
    )(xb, w_in, wmap, scale)


def _act_mm_kernel(x_ref, w_ref, o_ref, *, silu, sub_rows):
    w = w_ref[0].astype(BF16)
    for r in range(x_ref.shape[0] // sub_rows):
        rows = slice(r * sub_rows, (r + 1) * sub_rows)
        h = _dot(x_ref[rows, :], w)
        sg = _sigmoid(h)
        o_ref[rows, :] = (sg * h if silu else sg).astype(o_ref.dtype)


def _act_mm_call(xb2, w, layer, col0, n_cols, silu, tm, tn, name):
    t, d = xb2.shape
    return pl.pallas_call(
        functools.partial(_act_mm_kernel, silu=silu, sub_rows=min(512, tm)),
        grid=(t // tm, n_cols // tn),
        in_specs=[pl.BlockSpec((tm, d), lambda i, j: (i, 0)),
                  pl.BlockSpec((1, d, tn), lambda i, j: (layer, 0, col0 // tn + j))],
        out_specs=pl.BlockSpec((tm, tn), lambda i, j: (i, j)),
        out_shape=jax.ShapeDtypeStruct((t, n_cols), BF16),
        compiler_params=_cparams("parallel", "arbitrary"),
        name=name,
    )(xb2, w)


def _fill_padded(hp_ref, x_ref, w, seq, mm_rows, blocks):
    if 0 in blocks:
        zeros = jnp.zeros((CONV_PAD_ROWS, LANES), F32)
        for sl in range(hp_ref.shape[0]):
            hp_ref[sl, 0:CONV_PAD_ROWS, :] = zeros
            hp_ref[sl, CONV_PAD_ROWS + seq:2 * CONV_PAD_ROWS + seq, :] = zeros
    for r in blocks:
        res = _dot(x_ref[0, r * mm_rows:(r + 1) * mm_rows, :], w)
        for sl in range(hp_ref.shape[0]):
            hp_ref[sl, CONV_PAD_ROWS + r * mm_rows:CONV_PAD_ROWS + (r + 1) * mm_rows, :] = (
                res[:, sl * LANES:(sl + 1) * LANES])


def _conv_phases(hp_ref, sl, cw_ref, b_ref, row0, rows, taps, lanes):
    pad = taps // 2
    n = rows // CONV_PHASES
    shifted = {m: hp_ref[sl, pl.ds(CONV_PAD_ROWS + row0 + m, n, stride=CONV_PHASES), :]
               for m in range(-pad, CONV_PHASES + pad)}
    outs = []
    for j in range(CONV_PHASES):
        acc = b_ref[:, lanes]
        for k in range(taps):
            acc = acc + cw_ref[k:k + 1, lanes] * shifted[j + k - pad]
        outs.append(acc)
    return outs


def _store_phases(st_ref, sl, row0, rows, phases):
    for j, v in enumerate(phases):
        st_ref[sl, pl.ds(row0 + j, rows // CONV_PHASES, stride=CONV_PHASES), :] = v


def _xbc_kernel(x_ref, w_ref, cw_ref, b_ref, o_ref, *scratch, seq, taps, sub_cols):
    n_sub = len(scratch) // 2
    slabs = sub_cols // LANES
    mm_rows = min(512, seq)
    rows = min(256, seq)
    n_mm = seq // mm_rows
    ws = [w_ref[0, :, q * sub_cols:(q + 1) * sub_cols].astype(BF16) for q in range(n_sub)]

    def conv_block(q, r):
        hp_ref, st_ref = scratch[2 * q], scratch[2 * q + 1]
        for sl in range(slabs):
            gsl = q * slabs + sl
            lanes = slice(gsl * LANES, (gsl + 1) * LANES)
            phases = _conv_phases(hp_ref, sl, cw_ref, b_ref, r * rows, rows, taps, lanes)
            _store_phases(st_ref, sl, r * rows, rows, [v * _sigmoid(v) for v in phases])
            o_ref[0, r * rows:(r + 1) * rows, lanes] = (
                st_ref[sl, r * rows:(r + 1) * rows, :].astype(o_ref.dtype))

    _fill_padded(scratch[0], x_ref, ws[0], seq, mm_rows, range(n_mm))
    for q in range(n_sub):
        for m in range(n_mm):
            if q + 1 < n_sub:
                _fill_padded(scratch[2 * q + 2], x_ref, ws[q + 1], seq, mm_rows, [m])
            for r in range(m * mm_rows // rows, (m + 1) * mm_rows // rows):
                conv_block(q, r)


def _xbc_call(xb, w_in, layer, col0, conv_w, conv_b, tn):
    bsz, s, d = xb.shape
    n = conv_w.shape[1]
    taps = conv_w.shape[0]
    sub_cols = 2 * LANES
    sub_scratch = [pltpu.VMEM((sub_cols // LANES, s + 2 * CONV_PAD_ROWS, LANES), F32),
                   pltpu.VMEM((sub_cols // LANES, s, LANES), F32)]
    return pl.pallas_call(
        functools.partial(_xbc_kernel, seq=s, taps=taps, sub_cols=sub_cols),
        grid=(bsz, n // tn),
        in_specs=[pl.BlockSpec((1, s, d), lambda b, j: (b, 0, 0)),
                  pl.BlockSpec((1, d, tn), lambda b, j: (layer, 0, col0 // tn + j)),
                  pl.BlockSpec((taps, tn), lambda b, j: (0, j)),
                  pl.BlockSpec((1, tn), lambda b, j: (0, j))],
        out_specs=pl.BlockSpec((1, s, tn), lambda b, j: (b, 0, j)),
        out_shape=jax.ShapeDtypeStruct((bsz, s, n), BF16),
        scratch_shapes=sub_scratch * (tn // sub_cols),
        compiler_params=_cparams("parallel", "arbitrary"),
        name="xbc",
    )(xb, w_in, conv_w, conv_b)


def _ssd_kernel(x_ref, b_ref, c_ref, zs_ref, rows_ref, acol_ref, colfac_ref, e_ref,
                dskip_ref, gain_ref, o_ref, y_ref, xwf_ref, xwb_ref, scf_ref, scb_ref,
                etf_ref, etb_ref, sf_ref, sb_ref, *, n_chunks):
    L = SSD_CHUNK
    HP = HEAD_GROUP_LANES
    HPG = SSD_HEADS_PER_GROUP
    li = lax.broadcasted_iota(jnp.int32, (L, L), 0)
    si = lax.broadcasted_iota(jnp.int32, (L, L), 1)
    lower = si < li
    upper = si > li
    lane = lax.broadcasted_iota(jnp.int32, (L, 2 * SSD_HEAD_DIM), 1)
    first_head = lane < SSD_HEAD_DIM

    def chunk_local(c):
        rc = pl.ds(pl.multiple_of(c * L, L), L)
        xc = x_ref[0, rc, :]
        bc = b_ref[0, rc, :]
        cc = c_ref[0, rc, :]
        scores = lax.dot_general(cc, bc, (((1,), (1,)), ((), ())), preferred_element_type=F32)
        arow = rows_ref[0, c, 0]
        drow = rows_ref[0, c, 1]
        acol = acol_ref[0, 0, rc, :]
        ypairs = []
        for p in range(HPG // 2):
            dmats = []
            for h in (2 * p, 2 * p + 1):
                hb = HPG + h
                seg = jnp.where(lower, acol[:, h:h + 1] - arow[h:h + 1, :],
                                jnp.where(upper, acol[:, hb:hb + 1] - arow[hb:hb + 1, :],
                                          drow[h:h + 1, :]))
                dmats.append((scores * jnp.exp(seg)).astype(BF16))
            dpair = jnp.concatenate(dmats, axis=1)
            xpair = xc[:, p * 2 * SSD_HEAD_DIM:(p + 1) * 2 * SSD_HEAD_DIM]
            zero = jnp.zeros_like(xpair)
            xbd = jnp.concatenate([jnp.where(first_head, xpair, zero),
                                   jnp.where(first_head, zero, xpair)], axis=0)
            ypairs.append(_dot(dpair, xbd))
        y_ref[rc, :] = jnp.concatenate(ypairs, axis=1)

        fac = _dot(colfac_ref[0, 0, rc, :], e_ref[...])
        xf = xc.astype(F32)
        scf_ref[rc, :] = fac[:, 0 * HP:1 * HP]
        xwf_ref[rc, :] = (xf * fac[:, 1 * HP:2 * HP]).astype(BF16)
        scb_ref[rc, :] = fac[:, 2 * HP:3 * HP]
        xwb_ref[rc, :] = (xf * fac[:, 3 * HP:4 * HP]).astype(BF16)
        etf_ref[c] = jnp.broadcast_to(fac[L - 1:L, 0 * HP:1 * HP], (SUBLANES, HP))
        etb_ref[c] = jnp.broadcast_to(fac[0:1, 2 * HP:3 * HP], (SUBLANES, HP))

    def local_step(i, carry):
        for u in range(LOCAL_UNROLL):
            chunk_local(i * LOCAL_UNROLL + u)
        return carry

    lax.fori_loop(0, n_chunks // LOCAL_UNROLL, local_step, 0)

    sf_ref[...] = jnp.zeros(sf_ref.shape, F32)
    sb_ref[...] = jnp.zeros(sb_ref.shape, F32)

    def scan_one(c, s_ref, xw_ref, sc_ref, et_ref):
        rc = pl.ds(pl.multiple_of(c * L, L), L)
        s_old = s_ref[...]
        y_ref[rc, :] += _dot(c_ref[0, rc, :], s_old.astype(BF16)) * sc_ref[rc, :]
        upd = lax.dot_general(b_ref[0, rc, :], xw_ref[rc, :], (((0,), (0,)), ((), ())),
                              preferred_element_type=F32)
        s_ref[...] = s_old * et_ref[c][0:1, :] + upd

    def scan_step(i, carry):
        scan_one(i, sf_ref, xwf_ref, scf_ref, etf_ref)
        scan_one(n_chunks - 1 - i, sb_ref, xwb_ref, scb_ref, etb_ref)
        return carry

    lax.fori_loop(0, n_chunks, scan_step, 0)

    rows = 256 if (n_chunks * L) % 256 == 0 else L
    for r in range(n_chunks * L // rows):
        sl = slice(r * rows, (r + 1) * rows)
        y = y_ref[sl, :] + dskip_ref[...] * x_ref[0, sl, :].astype(F32)
        v = y * zs_ref[0, sl, :].astype(F32)
        v = v * lax.rsqrt(jnp.mean(v * v, axis=-1, keepdims=True) + RMS_EPS)
        o_ref[0, sl, :] = (v * gain_ref[...]).astype(o_ref.dtype)


def _ssd_call(xbc, zs, rows, acol, colfac, e_all, dskip, gain):
    bsz, s, _ = xbc.shape
    nc = s // SSD_CHUNK
    assert nc % LOCAL_UNROLL == 0
    HP = HEAD_GROUP_LANES
    n_x_blocks = SSD_HEADS * SSD_HEAD_DIM // SSD_STATE
    return pl.pallas_call(
        functools.partial(_ssd_kernel, n_chunks=nc),
        grid=(bsz, SSD_GROUPS),
        in_specs=[pl.BlockSpec((1, s, HP), lambda b, g: (b, 0, g)),
                  pl.BlockSpec((1, s, SSD_STATE), lambda b, g: (b, 0, n_x_blocks + g)),
                  pl.BlockSpec((1, s, SSD_STATE), lambda b, g: (b, 0, n_x_blocks + SSD_GROUPS + g)),
                  pl.BlockSpec((1, s, HP), lambda b, g: (b, 0, g)),
                  pl.BlockSpec((1, nc, 2, DIRS_X_HEADS, SSD_CHUNK), lambda b, g: (b, 0, 0, g, 0)),
                  pl.BlockSpec((1, 1, s, LANES), lambda b, g: (b, g, 0, 0)),
                  pl.BlockSpec((1, 1, s, LANES), lambda b, g: (b, g, 0, 0)),
                  pl.BlockSpec((LANES, 4 * HP), lambda b, g: (0, 0)),
                  pl.BlockSpec((1, HP), lambda b, g: (0, g)),
                  pl.BlockSpec((1, HP), lambda b, g: (0, g))],
        out_specs=pl.BlockSpec((1, s, HP), lambda b, g: (b, 0, g)),
        out_shape=jax.ShapeDtypeStruct((bsz, s, SSD_GROUPS * HP), BF16),
        scratch_shapes=[pltpu.VMEM((s, HP), F32),
                        pltpu.VMEM((s, HP), BF16), pltpu.VMEM((s, HP), BF16),
                        pltpu.VMEM((s, HP), F32), pltpu.VMEM((s, HP), F32),
                        pltpu.VMEM((nc, SUBLANES, HP), F32), pltpu.VMEM((nc, SUBLANES, HP), F32),
                        pltpu.VMEM((SSD_STATE, HP), F32), pltpu.VMEM((SSD_STATE, HP), F32)],
        compiler_params=_cparams("parallel", "arbitrary"),
        name="ssd",
    )(xbc, xbc, xbc, zs, rows, acol, colfac, e_all, dskip, gain)


def _merge_kernel(yn_ref, pool_ref, g0_ref, g1_ref, x_ref, wsp32_ref, wout32_ref, lng_ref, lnb_ref,
                  of_ref, ob_ref, wsp_ref, wout_ref, *, alpha):
    @pl.when(pl.program_id(0) == 0)
    def _():
        wsp_ref[...] = wsp32_ref[0].astype(BF16)
        wout_ref[...] = wout32_ref[0].astype(BF16)

    ssd_out = _dot(yn_ref[...], wsp_ref[...])
    merged = g0_ref[...].astype(F32) * pool_ref[...].astype(F32) + g1_ref[...].astype(F32) * ssd_out
    mix = _dot(merged.astype(BF16), wout_ref[...])
    out = _layer_norm(alpha * x_ref[...] + mix, lng_ref[...], lnb_ref[...])
    of_ref[...] = out
    ob_ref[...] = out.astype(BF16)


def _merge_call(yn, pool_out, gates, x, wsp, wout, layer, lng, lnb, alpha, tm):
    t, d = x.shape
    inner = yn.shape[1]
    row = lambda i: (i, 0)
    const = lambda i: (0, 0)
    resident = lambda shape: pl.BlockSpec(shape, lambda i: (layer, 0, 0), pipeline_mode=pl.Buffered(1))
    return pl.pallas_call(
        functools.partial(_merge_kernel, alpha=alpha),
        grid=(t // tm,),
        in_specs=[pl.BlockSpec((tm, inner), row),
                  pl.BlockSpec((tm, d), row),
                  pl.BlockSpec((tm, d), lambda i: (i, 0)),
                  pl.BlockSpec((tm, d), lambda i: (i, 1)),
                  pl.BlockSpec((tm, d), row),
                  resident((1, inner, d)),
                  resident((1, d, d)),
                  pl.BlockSpec((1, d), const),
                  pl.BlockSpec((1, d), const)],
        out_specs=[pl.BlockSpec((tm, d), row), pl.BlockSpec((tm, d), row)],
        out_shape=[jax.ShapeDtypeStruct((t, d), F32), jax.ShapeDtypeStruct((t, d), BF16)],
        scratch_shapes=[pltpu.VMEM((inner, d), BF16), pltpu.VMEM((d, d), BF16)],
        compiler_params=_cparams("arbitrary"),
        name="merge",
    )(yn, pool_out, gates, gates, x, wsp, wout, lng, lnb)


def _ffn_up_kernel(x_ref, wg_ref, wv_ref, cwg_ref, cwv_ref, bg_ref, bv_ref, o_ref, hg_ref, hv_ref,
                   st_ref, *, seq, taps):
    mm_rows = min(512, seq)
    _fill_padded(hg_ref, x_ref, wg_ref[0].astype(BF16), seq, mm_rows, range(seq // mm_rows))
    _fill_padded(hv_ref, x_ref, wv_ref[0].astype(BF16), seq, mm_rows, range(seq // mm_rows))
    rows = min(256, seq)
    for r in range(seq // rows):
        for sl in range(hg_ref.shape[0]):
            lanes = slice(sl * LANES, (sl + 1) * LANES)
            gates = _conv_phases(hg_ref, sl, cwg_ref, bg_ref, r * rows, rows, taps, lanes)
            vals = _conv_phases(hv_ref, sl, cwv_ref, bv_ref, r * rows, rows, taps, lanes)
            acts = [0.5 * g * (1.0 + lax.erf(g * np.float32(2.0 ** -0.5))) * v
                    for g, v in zip(gates, vals)]
            _store_phases(st_ref, sl, r * rows, rows, acts)
            o_ref[0, r * rows:(r + 1) * rows, sl * LANES:(sl + 1) * LANES] = (
                st_ref[sl, r * rows:(r + 1) * rows, :].astype(o_ref.dtype))


def _ffn_up_call(xb, w_up, layer, conv_w, conv_b, tn):
    bsz, s, d = xb.shape
    dff = w_up.shape[2] // 2
    nt = dff // tn
    taps = conv_w.shape[0]
    return pl.pallas_call(
        functools.partial(_ffn_up_kernel, seq=s, taps=taps),
        grid=(bsz, nt),
        in_specs=[pl.BlockSpec((1, s, d), lambda b, j: (b, 0, 0)),
                  pl.BlockSpec((1, d, tn), lambda b, j: (layer, 0, j)),
                  pl.BlockSpec((1, d, tn), lambda b, j: (layer, 0, nt + j)),
                  pl.BlockSpec((taps, tn), lambda b, j: (0, j)),
                  pl.BlockSpec((taps, tn), lambda b, j: (0, nt + j)),
                  pl.BlockSpec((1, tn), lambda b, j: (0, j)),
                  pl.BlockSpec((1, tn), lambda b, j: (0, nt + j))],
        out_specs=pl.BlockSpec((1, s, tn), lambda b, j: (b, 0, j)),
        out_shape=jax.ShapeDtypeStruct((bsz, s, dff), BF16),
        scratch_shapes=[pltpu.VMEM((tn // LANES, s + 2 * CONV_PAD_ROWS, LANES), F32),
                        pltpu.VMEM((tn // LANES, s + 2 * CONV_PAD_ROWS, LANES), F32),
                        pltpu.VMEM((tn // LANES, s, LANES), F32)],
        compiler_params=_cparams("parallel", "arbitrary"),
        name="ffn_up",
    )(xb, w_up, w_up, conv_w, conv_w, conv_b, conv_b)


def _ffn_down_kernel(a_ref, x_ref, w32_ref, lng_ref, lnb_ref, of_ref, ob_ref, w_ref, *, alpha):
    @pl.when(pl.program_id(0) == 0)
    def _():
        w_ref[...] = w32_ref[0].astype(BF16)

    out = _layer_norm(alpha * x_ref[...] + _dot(a_ref[...], w_ref[...]), lng_ref[...], lnb_ref[...])
    of_ref[...] = out
    ob_ref[...] = out.astype(BF16)


def _ffn_down_call(act, x, w_down, layer, lng, lnb, alpha, tm):
    t, d = x.shape
    dff = act.shape[1]
    row = lambda i: (i, 0)
    const = lambda i: (0, 0)
    return pl.pallas_call(
        functools.partial(_ffn_down_kernel, alpha=alpha),
        grid=(t // tm,),
        in_specs=[pl.BlockSpec((tm, dff), row), pl.BlockSpec((tm, d), row),
                  pl.BlockSpec((1, dff, d), lambda i: (layer, 0, 0), pipeline_mode=pl.Buffered(1)),
                  pl.BlockSpec((1, d), const), pl.BlockSpec((1, d), const)],
        out_specs=[pl.BlockSpec((tm, d), row), pl.BlockSpec((tm, d), row)],
        out_shape=[jax.ShapeDtypeStruct((t, d), F32), jax.ShapeDtypeStruct((t, d), BF16)],
        scratch_shapes=[pltpu.VMEM((dff, d), BF16)],
        compiler_params=_cparams("arbitrary"),
        name="ffn_down",
    )(act, x, w_down, lng, lnb)


def _placement_constants():
    nh = 2 * SSD_HEADS
    pa = np.zeros((SSD_GROUPS, nh, LANES), np.float32)
    pe = np.zeros((2, SSD_GROUPS, nh, LANES), np.float32)
    pw = np.zeros((2, SSD_GROUPS, nh, LANES), np.float32)
    for g in range(SSD_GROUPS):
        for j in range(DIRS_X_HEADS):
            col = g * DIRS_X_HEADS + j
            pa[g, col, j] = 1.0
            for part in range(2):
                pe[part, g, col, part * 2 * DIRS_X_HEADS + j] = 1.0
                pw[part, g, col, part * 2 * DIRS_X_HEADS + DIRS_X_HEADS + j] = 1.0
    HP = HEAD_GROUP_LANES
    ef = np.zeros((LANES, 2 * HP), np.float32)
    eb = np.zeros((LANES, 2 * HP), np.float32)
    for part in range(2):
        base = part * 2 * DIRS_X_HEADS
        for h in range(SSD_HEADS_PER_GROUP):
            cols = slice(h * SSD_HEAD_DIM, (h + 1) * SSD_HEAD_DIM)
            cols_w = slice(HP + h * SSD_HEAD_DIM, HP + (h + 1) * SSD_HEAD_DIM)
            ef[base + h, cols] = 1.0
            eb[base + SSD_HEADS_PER_GROUP + h, cols] = 1.0
            ef[base + DIRS_X_HEADS + h, cols_w] = 1.0
            eb[base + DIRS_X_HEADS + SSD_HEADS_PER_GROUP + h, cols_w] = 1.0
    as_bf16 = lambda a: jnp.asarray(a, BF16)
    return as_bf16(pa), as_bf16(pe), as_bf16(pw), as_bf16(np.concatenate([ef, eb], axis=1))


def _dt_column_order():
    order = []
    for g in range(SSD_GROUPS):
        for d in range(2):
            for h in range(SSD_HEADS_PER_GROUP):
                order.append(d * SSD_HEADS + g * SSD_HEADS_PER_GROUP + h)
    return np.asarray(order, np.int32)


def kernel(x, w_in, pool_w, pool_scale, ssd_conv_w, ssd_conv_b, a_log, dt_bias, d_skip, ssd_norm_g,
           w_ssd_proj, w_out, ln1_g, ln1_b, w_up, ffn_conv_w, ffn_conv_b, w_down, ln2_g, ln2_b):
    bsz, s, d = x.shape
    depth = w_in.shape[0]
    t = bsz * s
    alpha = float((2 * depth) ** 0.25)
    pool_width = pool_w.shape[1] * pool_w.shape[2]
    inner = w_ssd_proj.shape[1]
    conv_ch = ssd_conv_w.shape[2]
    n_dt = 2 * SSD_HEADS
    c0, c1, c2, c3 = pool_width, pool_width + inner, pool_width + inner + conv_ch, \
        pool_width + inner + conv_ch + n_dt
    tm = min(512, t)
    tm_zg = min(2048, t)
    order = _dt_column_order()
    pa, pe, pw, e_all = _placement_constants()

    xf = x.reshape(t, d)
    xb = xf.astype(BF16)
    for i in range(depth):
        w_gate = w_in[i, :, c3:].astype(BF16)[None]
        w_dt = w_in[i, :, c2:c3][:, order]
        xb3 = xb.reshape(bsz, s, d)

        dt_b = dt_bias[i].reshape(-1)[order]
        al = a_log[i].reshape(-1)[order]
        rows, acol, colfac = _dt_call(
            xb3, w_dt.astype(BF16), w_dt.T.astype(BF16), dt_b.reshape(1, -1), dt_b.reshape(-1, 1),
            al.reshape(1, -1), al.reshape(-1, 1), pa, pe, pw)
        pool_out = _pool_call(xb3, w_in, i, pool_w, pool_scale[i].reshape(1, -1))
        zs = _act_mm_call(xb, w_in, i, c0, inner, True, tm_zg, 512, "z")
        gates = _act_mm_call(xb, w_gate, 0, 0, 2 * d, False, tm_zg, 512, "gates")
        xbc = _xbc_call(xb3, w_in, i, c1, ssd_conv_w[i], ssd_conv_b[i].reshape(1, -1), 512)
        yn = _ssd_call(xbc, zs.reshape(bsz, s, -1), rows, acol, colfac, e_all,
                       jnp.repeat(d_skip[i], SSD_HEAD_DIM).reshape(1, -1),
                       ssd_norm_g[i].reshape(1, -1))
        xf, xb = _merge_call(yn.reshape(t, inner), pool_out.reshape(t, d), gates, xf,
                             w_ssd_proj, w_out, i, ln1_g[i].reshape(1, -1), ln1_b[i].reshape(1, -1),
                             alpha, tm)
        act = _ffn_up_call(xb.reshape(bsz, s, d), w_up, i, ffn_conv_w[i],
                           ffn_conv_b[i].reshape(1, -1), 256)
        xf, xb = _ffn_down_call(act.reshape(t, -1), xf, w_down, i,
                                ln2_g[i].reshape(1, -1), ln2_b[i].reshape(1, -1), alpha, tm)
    return xf.reshape(bsz, s, d)
```

```python
import functools

import numpy as np
import jax
import jax.numpy as jnp
from jax import lax
from jax.experimental import pallas as pl
from jax.experimental.pallas import tpu as pltpu

F32 = jnp.float32
BF16 = jnp.bfloat16

POOL_WINDOWS = (2, 4, 8, 16)
N_POOL_GROUPS = 4
SSD_HEAD_DIM = 64
SSD_GROUPS = 4
SSD_HEADS_PER_GROUP = 8
SSD_HEADS = SSD_GROUPS * SSD_HEADS_PER_GROUP
SSD_STATE = 128
SSD_CHUNK = 128
LN_EPS = 1e-5
RMS_EPS = 1e-5

LANES = 128
SUBLANES = 8
VMEM_LIMIT_BYTES = 56 * 1024 * 1024

CONV_PAD_ROWS = SUBLANES
CONV_PHASES = 4
POOL_BLOCK = 256
POOL_HALO = 16
HEAD_GROUP_LANES = SSD_HEADS_PER_GROUP * SSD_HEAD_DIM
DIRS_X_HEADS = 2 * SSD_HEADS_PER_GROUP
LOCAL_UNROLL = 8
SCAN_UNROLL = 4
ROW_SUB_BLOCK = 256


def _cparams(*sem):
    return pltpu.CompilerParams(dimension_semantics=sem, vmem_limit_bytes=VMEM_LIMIT_BYTES)


def _sigmoid(v):
    return 1.0 / (1.0 + jnp.exp(-v))


def _softplus(v):
    return jnp.maximum(v, 0.0) + jnp.log1p(jnp.exp(-jnp.abs(v)))


def _dot(a, b):
    return jnp.dot(a, b, preferred_element_type=F32)


def _split3(v):
    hi = v.astype(BF16)
    r1 = v - hi.astype(F32)
    mid = r1.astype(BF16)
    lo = (r1 - mid.astype(F32)).astype(BF16)
    return hi, mid, lo


def _dot_exact_rhs(m, v):
    hi, mid, lo = _split3(v)
    return _dot(m, hi) + _dot(m, mid) + _dot(m, lo)


def _dot_exact_lhs(v, m):
    hi, mid, lo = _split3(v)
    return _dot(hi, m) + _dot(mid, m) + _dot(lo, m)


def _layer_norm(r, g, b):
    mu = jnp.mean(r, axis=-1, keepdims=True)
    d = r - mu
    var = jnp.mean(d * d, axis=-1, keepdims=True)
    return d * lax.rsqrt(var + LN_EPS) * g + b


def _dt_kernel(x_ref, wdt_ref, wdtT_ref, brow_ref, bcol_ref, alrow_ref, alcol_ref,
               pa_ref, pe_ref, pw_ref, rows_ref, acol_ref, colfac_ref, *, n_chunks):
    L = SSD_CHUNK
    x = x_ref[0]
    dt_c = _softplus(_dot(x, wdt_ref[...]) + brow_ref[...])
    dt_r = _softplus(lax.dot_general(wdtT_ref[...], x, (((1,), (1,)), ((), ())),
                                     preferred_element_type=F32) + bcol_ref[...])
    adt_c = dt_c * (-jnp.exp(alrow_ref[...]))
    adt_r = dt_r * (-jnp.exp(alcol_ref[...]))
    log_dt_r = jnp.log(dt_r)
    hpg = SSD_HEADS_PER_GROUP
    pieces = []
    for g in range(SSD_GROUPS):
        both = jnp.log(dt_r[2 * g * hpg:(2 * g + 1) * hpg] + dt_r[(2 * g + 1) * hpg:(2 * g + 2) * hpg])
        pieces += [both, both]
    log_dt_both = jnp.concatenate(pieces, axis=0)
    ri = lax.broadcasted_iota(jnp.int32, (L, L), 0)
    ci = lax.broadcasted_iota(jnp.int32, (L, L), 1)
    lower = (ri >= ci).astype(BF16)
    upper = (ri <= ci).astype(BF16)
    nh = dt_c.shape[1]
    isb_row = (lax.broadcasted_iota(jnp.int32, (1, nh), 1) // SSD_HEADS_PER_GROUP) % 2 == 1
    isb_col = (lax.broadcasted_iota(jnp.int32, (nh, 1), 0) // SSD_HEADS_PER_GROUP) % 2 == 1
    for c in range(n_chunks):
        sl = slice(c * L, (c + 1) * L)
        a_c = adt_c[sl]
        cum_c = jnp.where(isb_row, _dot_exact_rhs(upper, a_c), _dot_exact_rhs(lower, a_c))
        tot = jnp.where(isb_row, cum_c[0:1], cum_c[L - 1:L])
        e_a = jnp.exp(cum_c)
        w_in = dt_c[sl] * jnp.exp(tot - cum_c)
        e_hi = e_a.astype(BF16)
        e_lo = (e_a - e_hi.astype(F32)).astype(BF16)
        w_hi = w_in.astype(BF16)
        w_lo = (w_in - w_hi.astype(F32)).astype(BF16)
        acol_all = _dot_exact_lhs(cum_c, pa_ref[...])
        colfac_all = (_dot(e_hi, pe_ref[0]) + _dot(w_hi, pw_ref[0])
                      + _dot(e_lo, pe_ref[1]) + _dot(w_lo, pw_ref[1])).astype(BF16)
        for g in range(SSD_GROUPS):
            acol_ref[0, g, sl, :] = acol_all[:, g * LANES:(g + 1) * LANES]
            colfac_ref[0, g, sl, :] = colfac_all[:, g * LANES:(g + 1) * LANES]
        a_r = adt_r[:, sl]
        cum_r = jnp.where(isb_col, _dot_exact_lhs(a_r, lower), _dot_exact_lhs(a_r, upper))
        rows_ref[0, c, 0] = cum_r - log_dt_r[:, sl]
        rows_ref[0, c, 1] = log_dt_both[:, sl]


def _dt_call(xb, wdt, wdtT, brow, bcol, alrow, alcol, pa, pe, pw):
    bsz, s, d = xb.shape
    nc = s // SSD_CHUNK
    nh = 2 * SSD_HEADS
    const = lambda *shape: pl.BlockSpec(shape, lambda b: (0,) * len(shape))
    return pl.pallas_call(
        functools.partial(_dt_kernel, n_chunks=nc),
        grid=(bsz,),
        in_specs=[pl.BlockSpec((1, s, d), lambda b: (b, 0, 0)),
                  const(d, nh), const(nh, d), const(1, nh), const(nh, 1), const(1, nh), const(nh, 1),
                  const(nh, SSD_GROUPS * LANES), const(2, nh, SSD_GROUPS * LANES),
                  const(2, nh, SSD_GROUPS * LANES)],
        out_specs=[pl.BlockSpec((1, nc, 2, nh, SSD_CHUNK), lambda b: (b, 0, 0, 0, 0)),
                   pl.BlockSpec((1, SSD_GROUPS, s, LANES), lambda b: (b, 0, 0, 0)),
                   pl.BlockSpec((1, SSD_GROUPS, s, LANES), lambda b: (b, 0, 0, 0))],
        out_shape=[jax.ShapeDtypeStruct((bsz, nc, 2, nh, SSD_CHUNK), F32),
                   jax.ShapeDtypeStruct((bsz, SSD_GROUPS, s, LANES), F32),
                   jax.ShapeDtypeStruct((bsz, SSD_GROUPS, s, LANES), BF16)],
        compiler_params=_cparams("parallel"),
        name="dt",
    )(xb, wdt, wdtT, brow, bcol, alrow, alcol, pa, pe, pw)


def _pool_kernel(x_ref, w_ref, wmap_ref, scale_ref, o_ref, u_ref, ub_ref, *, seq):
    g = pl.program_id(1)
    half = jnp.left_shift(1, g)
    u = _dot(x_ref[0], w_ref[0].astype(BF16))
    u_ref[...] = u
    ub_ref[...] = u.astype(BF16)
    blk = min(POOL_BLOCK, seq)
    win = min(seq, blk + 2 * POOL_HALO)
    for r in range(seq // blk):
        start = max(0, min(r * blk - POOL_HALO, seq - win))
        t = r * blk + lax.broadcasted_iota(jnp.int32, (blk, win), 0)
        j = start + lax.broadcasted_iota(jnp.int32, (blk, win), 1)
        band = ((j >= t - half) & (j <= t + half - 1)).astype(BF16)
        ssum = _dot(band, ub_ref[start:start + win, :])
        tq = r * blk + lax.broadcasted_iota(jnp.int32, ssum.shape, 0)
        cnt = jnp.minimum(tq + half - 1, seq - 1) - jnp.maximum(tq - half, 0) + 1
        pooled = ssum / cnt.astype(F32) - u_ref[r * blk:(r + 1) * blk, :]
        out = _dot(pooled.astype(BF16), wmap_ref[0, 0].astype(BF16)) * scale_ref[...]
        o_ref[0, r * blk:(r + 1) * blk, :] = out.astype(o_ref.dtype)


def _pool_call(xb, w_in, layer, wmap, scale):
    bsz, s, d = xb.shape
    pg = wmap.shape[-1]
    return pl.pallas_call(
        functools.partial(_pool_kernel, seq=s),
        grid=(bsz, N_POOL_GROUPS),
        in_specs=[pl.BlockSpec((1, s, d), lambda b, g: (b, 0, 0)),
                  pl.BlockSpec((1, d, pg), lambda b, g: (layer, 0, g)),
                  pl.BlockSpec((1, 1, pg, pg), lambda b, g: (layer, g, 0, 0)),
                  pl.BlockSpec((1, pg), lambda b, g: (0, g))],
        out_specs=pl.BlockSpec((1, s, pg), lambda b, g: (b, 0, g)),
        out_shape=jax.ShapeDtypeStruct((bsz, s, N_POOL_GROUPS * pg), BF16),
        scratch_shapes=[pltpu.VMEM((s, pg), F32), pltpu.VMEM((s, pg), BF16)],
        compiler_params=_cparams("parallel", "arbitrary"),
        name="pool",
    )(xb, w_in, wmap, scale)


def _act_mm_kernel(x_ref, w_ref, o_ref, *, silu, sub_rows):
    w = w_ref[0].astype(BF16)
    for r in range(x_ref.shape[0] // sub_rows):
        rows = slice(r * sub_rows, (r + 1) * sub_rows)
        h = _dot(x_ref[rows, :], w)
        sg = _sigmoid(h)
        o_ref[rows, :] = (sg * h if silu else sg).astype(o_ref.dtype)


def _act_mm_call(xb2, w, layer, col0, n_cols, silu, tm, tn, name):
    t, d = xb2.shape
    return pl.pallas_call(
        functools.partial(_act_mm_kernel, silu=silu, sub_rows=min(ROW_SUB_BLOCK, tm)),
        grid=(t // tm, n_cols // tn),
        in_specs=[pl.BlockSpec((tm, d), lambda i, j: (i, 0)),
                  pl.BlockSpec((1, d, tn), lambda i, j: (layer, 0, col0 // tn + j))],
        out_specs=pl.BlockSpec((tm, tn), lambda i, j: (i, j)),
        out_shape=jax.ShapeDtypeStruct((t, n_cols), BF16),
        compiler_params=_cparams("parallel", "arbitrary"),
        name=name,
    )(xb2, w)


def _fill_padded(hp_ref, x_ref, w, seq, mm_rows, blocks):
    if 0 in blocks:
        zeros = jnp.zeros((CONV_PAD_ROWS, LANES), F32)
        for sl in range(hp_ref.shape[0]):
            hp_ref[sl, 0:CONV_PAD_ROWS, :] = zeros
            hp_ref[sl, CONV_PAD_ROWS + seq:2 * CONV_PAD_ROWS + seq, :] = zeros
    for r in blocks:
        res = _dot(x_ref[0, r * mm_rows:(r + 1) * mm_rows, :], w)
        for sl in range(hp_ref.shape[0]):
            hp_ref[sl, CONV_PAD_ROWS + r * mm_rows:CONV_PAD_ROWS + (r + 1) * mm_rows, :] = (
                res[:, sl * LANES:(sl + 1) * LANES])


def _conv_phases(hp_ref, sl, cw_ref, b_ref, row0, rows, taps, lanes):
    pad = taps // 2
    n = rows // CONV_PHASES
    shifted = {m: hp_ref[sl, pl.ds(CONV_PAD_ROWS + row0 + m, n, stride=CONV_PHASES), :]
               for m in range(-pad, CONV_PHASES + pad)}
    outs = []
    for j in range(CONV_PHASES):
        acc = b_ref[:, lanes]
        for k in range(taps):
            acc = acc + cw_ref[k:k + 1, lanes] * shifted[j + k - pad]
        outs.append(acc)
    return outs


def _store_phases(st_ref, sl, row0, rows, phases):
    for j, v in enumerate(phases):
        st_ref[sl, pl.ds(row0 + j, rows // CONV_PHASES, stride=CONV_PHASES), :] = v


def _xbc_kernel(x_ref, w_ref, cw_ref, b_ref, o_ref, *scratch, seq, taps, sub_cols):
    n_sub = len(scratch) // 2
    slabs = sub_cols // LANES
    mm_rows = min(512, seq)
    rows = min(256, seq)
    n_mm = seq // mm_rows
    ws = [w_ref[0, :, q * sub_cols:(q + 1) * sub_cols].astype(BF16) for q in range(n_sub)]

    def conv_block(q, r):
        hp_ref, st_ref = scratch[2 * q], scratch[2 * q + 1]
        for sl in range(slabs):
            gsl = q * slabs + sl
            lanes = slice(gsl * LANES, (gsl + 1) * LANES)
            phases = _conv_phases(hp_ref, sl, cw_ref, b_ref, r * rows, rows, taps, lanes)
            _store_phases(st_ref, sl, r * rows, rows, [v * _sigmoid(v) for v in phases])
            o_ref[0, r * rows:(r + 1) * rows, lanes] = (
                st_ref[sl, r * rows:(r + 1) * rows, :].astype(o_ref.dtype))

    _fill_padded(scratch[0], x_ref, ws[0], seq, mm_rows, range(n_mm))
    for q in range(n_sub):
        for m in range(n_mm):
            if q + 1 < n_sub:
                _fill_padded(scratch[2 * q + 2], x_ref, ws[q + 1], seq, mm_rows, [m])
            for r in range(m * mm_rows // rows, (m + 1) * mm_rows // rows):
                conv_block(q, r)


def _xbc_call(xb, w_in, layer, col0, conv_w, conv_b, tn):
    bsz, s, d = xb.shape
    n = conv_w.shape[1]
    taps = conv_w.shape[0]
    sub_cols = 2 * LANES
    sub_scratch = [pltpu.VMEM((sub_cols // LANES, s + 2 * CONV_PAD_ROWS, LANES), F32),
                   pltpu.VMEM((sub_cols // LANES, s, LANES), F32)]
    return pl.pallas_call(
        functools.partial(_xbc_kernel, seq=s, taps=taps, sub_cols=sub_cols),
        grid=(bsz, n // tn),
        in_specs=[pl.BlockSpec((1, s, d), lambda b, j: (b, 0, 0)),
                  pl.BlockSpec((1, d, tn), lambda b, j: (layer, 0, col0 // tn + j)),
                  pl.BlockSpec((taps, tn), lambda b, j: (0, j)),
                  pl.BlockSpec((1, tn), lambda b, j: (0, j))],
        out_specs=pl.BlockSpec((1, s, tn), lambda b, j: (b, 0, j)),
        out_shape=jax.ShapeDtypeStruct((bsz, s, n), BF16),
        scratch_shapes=sub_scratch * (tn // sub_cols),
        compiler_params=_cparams("parallel", "arbitrary"),
        name="xbc",
    )(xb, w_in, conv_w, conv_b)


def _ssd_kernel(x_ref, b_ref, c_ref, zs_ref, rows_ref, acol_ref, colfac_ref, e_ref,
                dskip_ref, gain_ref, o_ref, y_ref, xwf_ref, xwb_ref, scf_ref, scb_ref,
                etf_ref, etb_ref, sf_ref, sb_ref, *, n_chunks):
    L = SSD_CHUNK
    HP = HEAD_GROUP_LANES
    HPG = SSD_HEADS_PER_GROUP
    li = lax.broadcasted_iota(jnp.int32, (L, L), 0)
    si = lax.broadcasted_iota(jnp.int32, (L, L), 1)
    lower = si < li
    upper = si > li
    lane = lax.broadcasted_iota(jnp.int32, (L, 2 * SSD_HEAD_DIM), 1)
    first_head = lane < SSD_HEAD_DIM

    def chunk_local(c):
        rc = pl.ds(pl.multiple_of(c * L, L), L)
        xc = x_ref[0, rc, :]
        bc = b_ref[0, rc, :]
        cc = c_ref[0, rc, :]
        scores = lax.dot_general(cc, bc, (((1,), (1,)), ((), ())), preferred_element_type=F32)
        arow = rows_ref[0, c, 0]
        drow = rows_ref[0, c, 1]
        acol = acol_ref[0, 0, rc, :]
        ypairs = []
        for p in range(HPG // 2):
            dmats = []
            for h in (2 * p, 2 * p + 1):
                hb = HPG + h
                seg = jnp.where(lower, acol[:, h:h + 1] - arow[h:h + 1, :],
                                jnp.where(upper, acol[:, hb:hb + 1] - arow[hb:hb + 1, :],
                                          drow[h:h + 1, :]))
                dmats.append((scores * jnp.exp(seg)).astype(BF16))
            dpair = jnp.concatenate(dmats, axis=1)
            xpair = xc[:, p * 2 * SSD_HEAD_DIM:(p + 1) * 2 * SSD_HEAD_DIM]
            zero = jnp.zeros_like(xpair)
            xbd = jnp.concatenate([jnp.where(first_head, xpair, zero),
                                   jnp.where(first_head, zero, xpair)], axis=0)
            ypairs.append(_dot(dpair, xbd))
        y_ref[rc, :] = jnp.concatenate(ypairs, axis=1)

        fac = _dot(colfac_ref[0, 0, rc, :], e_ref[...])
        xf = xc.astype(F32)
        scf_ref[rc, :] = fac[:, 0 * HP:1 * HP]
        xwf_ref[rc, :] = (xf * fac[:, 1 * HP:2 * HP]).astype(BF16)
        scb_ref[rc, :] = fac[:, 2 * HP:3 * HP]
        xwb_ref[rc, :] = (xf * fac[:, 3 * HP:4 * HP]).astype(BF16)
        etf_ref[c] = jnp.broadcast_to(fac[L - 1:L, 0 * HP:1 * HP], (SUBLANES, HP))
        etb_ref[c] = jnp.broadcast_to(fac[0:1, 2 * HP:3 * HP], (SUBLANES, HP))

    local_unroll = LOCAL_UNROLL if n_chunks % LOCAL_UNROLL == 0 else 1

    def local_step(i, carry):
        for u in range(local_unroll):
            chunk_local(i * local_unroll + u)
        return carry

    lax.fori_loop(0, n_chunks // local_unroll, local_step, 0)

    sf_ref[...] = jnp.zeros(sf_ref.shape, F32)
    sb_ref[...] = jnp.zeros(sb_ref.shape, F32)

    def scan_one(c, s_ref, xw_ref, sc_ref, et_ref):
        rc = pl.ds(pl.multiple_of(c * L, L), L)
        s_old = s_ref[...]
        y_ref[rc, :] += _dot(c_ref[0, rc, :], s_old.astype(BF16)) * sc_ref[rc, :]
        upd = lax.dot_general(b_ref[0, rc, :], xw_ref[rc, :], (((0,), (0,)), ((), ())),
                              preferred_element_type=F32)
        s_ref[...] = s_old * et_ref[c][0:1, :] + upd

    def scan_step(i, carry):
        scan_one(i, sf_ref, xwf_ref, scf_ref, etf_ref)
        scan_one(n_chunks - 1 - i, sb_ref, xwb_ref, scb_ref, etb_ref)
        return carry

    lax.fori_loop(0, n_chunks, scan_step, 0, unroll=SCAN_UNROLL if n_chunks % SCAN_UNROLL == 0 else 1)

    rows = 256 if (n_chunks * L) % 256 == 0 else L
    for r in range(n_chunks * L // rows):
        sl = slice(r * rows, (r + 1) * rows)
        y = y_ref[sl, :] + dskip_ref[...] * x_ref[0, sl, :].astype(F32)
        v = y * zs_ref[0, sl, :].astype(F32)
        v = v * lax.rsqrt(jnp.mean(v * v, axis=-1, keepdims=True) + RMS_EPS)
        o_ref[0, sl, :] = (v * gain_ref[...]).astype(o_ref.dtype)


def _ssd_call(xbc, zs, rows, acol, colfac, e_all, dskip, gain):
    bsz, s, _ = xbc.shape
    nc = s // SSD_CHUNK
    HP = HEAD_GROUP_LANES
    n_x_blocks = SSD_HEADS * SSD_HEAD_DIM // SSD_STATE
    return pl.pallas_call(
        functools.partial(_ssd_kernel, n_chunks=nc),
        grid=(bsz, SSD_GROUPS),
        in_specs=[pl.BlockSpec((1, s, HP), lambda b, g: (b, 0, g)),
                  pl.BlockSpec((1, s, SSD_STATE), lambda b, g: (b, 0, n_x_blocks + g)),
                  pl.BlockSpec((1, s, SSD_STATE), lambda b, g: (b, 0, n_x_blocks + SSD_GROUPS + g)),
                  pl.BlockSpec((1, s, HP), lambda b, g: (b, 0, g)),
                  pl.BlockSpec((1, nc, 2, DIRS_X_HEADS, SSD_CHUNK), lambda b, g: (b, 0, 0, g, 0)),
                  pl.BlockSpec((1, 1, s, LANES), lambda b, g: (b, g, 0, 0)),
                  pl.BlockSpec((1, 1, s, LANES), lambda b, g: (b, g, 0, 0)),
                  pl.BlockSpec((LANES, 4 * HP), lambda b, g: (0, 0)),
                  pl.BlockSpec((1, HP), lambda b, g: (0, g)),
                  pl.BlockSpec((1, HP), lambda b, g: (0, g))],
        out_specs=pl.BlockSpec((1, s, HP), lambda b, g: (b, 0, g)),
        out_shape=jax.ShapeDtypeStruct((bsz, s, SSD_GROUPS * HP), BF16),
        scratch_shapes=[pltpu.VMEM((s, HP), F32),
                        pltpu.VMEM((s, HP), BF16), pltpu.VMEM((s, HP), BF16),
                        pltpu.VMEM((s, HP), F32), pltpu.VMEM((s, HP), F32),
                        pltpu.VMEM((nc, SUBLANES, HP), F32), pltpu.VMEM((nc, SUBLANES, HP), F32),
                        pltpu.VMEM((SSD_STATE, HP), F32), pltpu.VMEM((SSD_STATE, HP), F32)],
        compiler_params=_cparams("parallel", "arbitrary"),
        name="ssd",
    )(xbc, xbc, xbc, zs, rows, acol, colfac, e_all, dskip, gain)


def _merge_kernel(yn_ref, pool_ref, g0_ref, g1_ref, x_ref, wsp32_ref, wout32_ref, lng_ref, lnb_ref,
                  of_ref, ob_ref, wsp_ref, wout_ref, *, alpha, sub_rows):
    @pl.when(pl.program_id(0) == 0)
    def _():
        wsp_ref[...] = wsp32_ref[0].astype(BF16)
        wout_ref[...] = wout32_ref[0].astype(BF16)

    for r in range(x_ref.shape[0] // sub_rows):
        rows = slice(r * sub_rows, (r + 1) * sub_rows)
        ssd_out = _dot(yn_ref[rows, :], wsp_ref[...])
        merged = (g0_ref[rows, :].astype(F32) * pool_ref[rows, :].astype(F32)
                  + g1_ref[rows, :].astype(F32) * ssd_out)
        mix = _dot(merged.astype(BF16), wout_ref[...])
        out = _layer_norm(alpha * x_ref[rows, :] + mix, lng_ref[...], lnb_ref[...])
        of_ref[rows, :] = out
        ob_ref[rows, :] = out.astype(BF16)


def _merge_call(yn, pool_out, gates, x, wsp, wout, layer, lng, lnb, alpha, tm):
    t, d = x.shape
    inner = yn.shape[1]
    row = lambda i: (i, 0)
    const = lambda i: (0, 0)
    resident = lambda shape: pl.BlockSpec(shape, lambda i: (layer, 0, 0), pipeline_mode=pl.Buffered(1))
    return pl.pallas_call(
        functools.partial(_merge_kernel, alpha=alpha, sub_rows=min(ROW_SUB_BLOCK, tm)),
        grid=(t // tm,),
        in_specs=[pl.BlockSpec((tm, inner), row),
                  pl.BlockSpec((tm, d), row),
                  pl.BlockSpec((tm, d), lambda i: (i, 0)),
                  pl.BlockSpec((tm, d), lambda i: (i, 1)),
                  pl.BlockSpec((tm, d), row),
                  resident((1, inner, d)),
                  resident((1, d, d)),
                  pl.BlockSpec((1, d), const),
                  pl.BlockSpec((1, d), const)],
        out_specs=[pl.BlockSpec((tm, d), row), pl.BlockSpec((tm, d), row)],
        out_shape=[jax.ShapeDtypeStruct((t, d), F32), jax.ShapeDtypeStruct((t, d), BF16)],
        scratch_shapes=[pltpu.VMEM((inner, d), BF16), pltpu.VMEM((d, d), BF16)],
        compiler_params=_cparams("arbitrary"),
        name="merge",
    )(yn, pool_out, gates, gates, x, wsp, wout, lng, lnb)


def _ffn_up_kernel(x_ref, wg_ref, wv_ref, cwg_ref, cwv_ref, bg_ref, bv_ref, o_ref, hg_ref, hv_ref,
                   st_ref, *, seq, taps):
    mm_rows = min(512, seq)
    _fill_padded(hg_ref, x_ref, wg_ref[0].astype(BF16), seq, mm_rows, range(seq // mm_rows))
    _fill_padded(hv_ref, x_ref, wv_ref[0].astype(BF16), seq, mm_rows, range(seq // mm_rows))
    rows = min(256, seq)
    for r in range(seq // rows):
        for sl in range(hg_ref.shape[0]):
            lanes = slice(sl * LANES, (sl + 1) * LANES)
            gates = _conv_phases(hg_ref, sl, cwg_ref, bg_ref, r * rows, rows, taps, lanes)
            vals = _conv_phases(hv_ref, sl, cwv_ref, bv_ref, r * rows, rows, taps, lanes)
            acts = [0.5 * g * (1.0 + lax.erf(g * np.float32(2.0 ** -0.5))) * v
                    for g, v in zip(gates, vals)]
            _store_phases(st_ref, sl, r * rows, rows, acts)
            o_ref[0, r * rows:(r + 1) * rows, sl * LANES:(sl + 1) * LANES] = (
                st_ref[sl, r * rows:(r + 1) * rows, :].astype(o_ref.dtype))


def _ffn_up_call(xb, w_up, layer, conv_w, conv_b, tn):
    bsz, s, d = xb.shape
    dff = w_up.shape[2] // 2
    nt = dff // tn
    taps = conv_w.shape[0]
    return pl.pallas_call(
        functools.partial(_ffn_up_kernel, seq=s, taps=taps),
        grid=(bsz, nt),
        in_specs=[pl.BlockSpec((1, s, d), lambda b, j: (b, 0, 0)),
                  pl.BlockSpec((1, d, tn), lambda b, j: (layer, 0, j)),
                  pl.BlockSpec((1, d, tn), lambda b, j: (layer, 0, nt + j)),
                  pl.BlockSpec((taps, tn), lambda b, j: (0, j)),
                  pl.BlockSpec((taps, tn), lambda b, j: (0, nt + j)),
                  pl.BlockSpec((1, tn), lambda b, j: (0, j)),
                  pl.BlockSpec((1, tn), lambda b, j: (0, nt + j))],
        out_specs=pl.BlockSpec((1, s, tn), lambda b, j: (b, 0, j)),
        out_shape=jax.ShapeDtypeStruct((bsz, s, dff), BF16),
        scratch_shapes=[pltpu.VMEM((tn // LANES, s + 2 * CONV_PAD_ROWS, LANES), F32),
                        pltpu.VMEM((tn // LANES, s + 2 * CONV_PAD_ROWS, LANES), F32),
                        pltpu.VMEM((tn // LANES, s, LANES), F32)],
        compiler_params=_cparams("parallel", "arbitrary"),
        name="ffn_up",
    )(xb, w_up, w_up, conv_w, conv_w, conv_b, conv_b)


def _ffn_down_kernel(a_ref, x_ref, w32_ref, lng_ref, lnb_ref, of_ref, ob_ref, w_ref, *, alpha, sub_rows):
    @pl.when(pl.program_id(0) == 0)
    def _():
        w_ref[...] = w32_ref[0].astype(BF16)

    for r in range(x_ref.shape[0] // sub_rows):
        rows = slice(r * sub_rows, (r + 1) * sub_rows)
        out = _layer_norm(alpha * x_ref[rows, :] + _dot(a_ref[rows, :], w_ref[...]),
                          lng_ref[...], lnb_ref[...])
        of_ref[rows, :] = out
        ob_ref[rows, :] = out.astype(BF16)


def _ffn_down_call(act, x, w_down, layer, lng, lnb, alpha, tm):
    t, d = x.shape
    dff = act.shape[1]
    row = lambda i: (i, 0)
    const = lambda i: (0, 0)
    return pl.pallas_call(
        functools.partial(_ffn_down_kernel, alpha=alpha, sub_rows=min(ROW_SUB_BLOCK, tm)),
        grid=(t // tm,),
        in_specs=[pl.BlockSpec((tm, dff), row), pl.BlockSpec((tm, d), row),
                  pl.BlockSpec((1, dff, d), lambda i: (layer, 0, 0), pipeline_mode=pl.Buffered(1)),
                  pl.BlockSpec((1, d), const), pl.BlockSpec((1, d), const)],
        out_specs=[pl.BlockSpec((tm, d), row), pl.BlockSpec((tm, d), row)],
        out_shape=[jax.ShapeDtypeStruct((t, d), F32), jax.ShapeDtypeStruct((t, d), BF16)],
        scratch_shapes=[pltpu.VMEM((dff, d), BF16)],
        compiler_params=_cparams("arbitrary"),
        name="ffn_down",
    )(act, x, w_down, lng, lnb)


def _placement_constants():
    nh = 2 * SSD_HEADS
    pa = np.zeros((nh, SSD_GROUPS * LANES), np.float32)
    pe = np.zeros((2, nh, SSD_GROUPS * LANES), np.float32)
    pw = np.zeros((2, nh, SSD_GROUPS * LANES), np.float32)
    for g in range(SSD_GROUPS):
        for j in range(DIRS_X_HEADS):
            col = g * DIRS_X_HEADS + j
            lane0 = g * LANES
            pa[col, lane0 + j] = 1.0
            for part in range(2):
                pe[part, col, lane0 + part * 2 * DIRS_X_HEADS + j] = 1.0
                pw[part, col, lane0 + part * 2 * DIRS_X_HEADS + DIRS_X_HEADS + j] = 1.0
    HP = HEAD_GROUP_LANES
    ef = np.zeros((LANES, 2 * HP), np.float32)
    eb = np.zeros((LANES, 2 * HP), np.float32)
    for part in range(2):
        base = part * 2 * DIRS_X_HEADS
        for h in range(SSD_HEADS_PER_GROUP):
            cols = slice(h * SSD_HEAD_DIM, (h + 1) * SSD_HEAD_DIM)
            cols_w = slice(HP + h * SSD_HEAD_DIM, HP + (h + 1) * SSD_HEAD_DIM)
            ef[base + h, cols] = 1.0
            eb[base + SSD_HEADS_PER_GROUP + h, cols] = 1.0
            ef[base + DIRS_X_HEADS + h, cols_w] = 1.0
            eb[base + DIRS_X_HEADS + SSD_HEADS_PER_GROUP + h, cols_w] = 1.0
    as_bf16 = lambda a: jnp.asarray(a, BF16)
    return as_bf16(pa), as_bf16(pe), as_bf16(pw), as_bf16(np.concatenate([ef, eb], axis=1))


def _dt_column_order():
    order = []
    for g in range(SSD_GROUPS):
        for d in range(2):
            for h in range(SSD_HEADS_PER_GROUP):
                order.append(d * SSD_HEADS + g * SSD_HEADS_PER_GROUP + h)
    return np.asarray(order, np.int32)


def kernel(x, w_in, pool_w, pool_scale, ssd_conv_w, ssd_conv_b, a_log, dt_bias, d_skip, ssd_norm_g,
           w_ssd_proj, w_out, ln1_g, ln1_b, w_up, ffn_conv_w, ffn_conv_b, w_down, ln2_g, ln2_b):
    bsz, s, d = x.shape
    depth = w_in.shape[0]
    t = bsz * s
    alpha = float((2 * depth) ** 0.25)
    pool_width = pool_w.shape[1] * pool_w.shape[2]
    inner = w_ssd_proj.shape[1]
    conv_ch = ssd_conv_w.shape[2]
    n_dt = 2 * SSD_HEADS
    c0, c1, c2, c3 = pool_width, pool_width + inner, pool_width + inner + conv_ch, \
        pool_width + inner + conv_ch + n_dt
    tm = min(1024, t)
    tm_zg = min(2048, t)
    order = _dt_column_order()
    pa, pe, pw, e_all = _placement_constants()

    xf = x.reshape(t, d)
    xb = xf.astype(BF16)
    for i in range(depth):
        w_gate = w_in[i, :, c3:].astype(BF16)[None]
        w_dt = w_in[i, :, c2:c3][:, order]
        xb3 = xb.reshape(bsz, s, d)

        dt_b = dt_bias[i].reshape(-1)[order]
        al = a_log[i].reshape(-1)[order]
        rows, acol, colfac = _dt_call(
            xb3, w_dt.astype(BF16), w_dt.T.astype(BF16), dt_b.reshape(1, -1), dt_b.reshape(-1, 1),
            al.reshape(1, -1), al.reshape(-1, 1), pa, pe, pw)
        pool_out = _pool_call(xb3, w_in, i, pool_w, pool_scale[i].reshape(1, -1))
        zs = _act_mm_call(xb, w_in, i, c0, inner, True, tm_zg, 512, "z")
        gates = _act_mm_call(xb, w_gate, 0, 0, 2 * d, False, tm_zg, 512, "gates")
        xbc = _xbc_call(xb3, w_in, i, c1, ssd_conv_w[i], ssd_conv_b[i].reshape(1, -1), 512)
        yn = _ssd_call(xbc, zs.reshape(bsz, s, -1), rows, acol, colfac, e_all,
                       jnp.repeat(d_skip[i], SSD_HEAD_DIM).reshape(1, -1),
                       ssd_norm_g[i].reshape(1, -1))
        xf, xb = _merge_call(yn.reshape(t, inner), pool_out.reshape(t, d), gates, xf,
                             w_ssd_proj, w_out, i, ln1_g[i].reshape(1, -1), ln1_b[i].reshape(1, -1),
                             alpha, min(512, t))
        act = _ffn_up_call(xb.reshape(bsz, s, d), w_up, i, ffn_conv_w[i],
                           ffn_conv_b[i].reshape(1, -1), 256)
        xf, xb = _ffn_down_call(act.reshape(t, -1), xf, w_down, i,
                                ln2_g[i].reshape(1, -1), ln2_b[i].reshape(1, -1), alpha, tm)
    return xf.reshape(bsz, s, d)
```

```python
import functools

import numpy as np
import jax
import jax.numpy as jnp
from jax import lax
from jax.experimental import pallas as pl
from jax.experimental.pallas import tpu as pltpu

F32 = jnp.float32
BF16 = jnp.bfloat16

POOL_WINDOWS = (2, 4, 8, 16)
N_POOL_GROUPS = 4
SSD_HEAD_DIM = 64
SSD_GROUPS = 4
SSD_HEADS_PER_GROUP = 8
SSD_HEADS = SSD_GROUPS * SSD_HEADS_PER_GROUP
SSD_STATE = 128
SSD_CHUNK = 128
LN_EPS = 1e-5
RMS_EPS = 1e-5

LANES = 128
SUBLANES = 8
VMEM_LIMIT_BYTES = 56 * 1024 * 1024

CONV_PAD_ROWS = SUBLANES
CONV_PHASES = 4
POOL_BLOCK = 256
POOL_HALO = 16
HEAD_GROUP_LANES = SSD_HEADS_PER_GROUP * SSD_HEAD_DIM
DIRS_X_HEADS = 2 * SSD_HEADS_PER_GROUP
LOCAL_UNROLL = 8
SCAN_UNROLL = 4
ROW_SUB_BLOCK = 256


def _cparams(*sem):
    return pltpu.CompilerParams(dimension_semantics=sem, vmem_limit_bytes=VMEM_LIMIT_BYTES)


def _sigmoid(v):
    return 1.0 / (1.0 + jnp.exp(-v))


def _softplus(v):
    return jnp.maximum(v, 0.0) + jnp.log1p(jnp.exp(-jnp.abs(v)))


def _dot(a, b):
    return jnp.dot(a, b, preferred_element_type=F32)


def _dot_t(a, b_t):
    return lax.dot_general(a, b_t, (((1,), (1,)), ((), ())), preferred_element_type=F32)


def _split3(v):
    hi = v.astype(BF16)
    r1 = v - hi.astype(F32)
    mid = r1.astype(BF16)
    lo = (r1 - mid.astype(F32)).astype(BF16)
    return hi, mid, lo


def _dot_exact_rhs(m, v):
    hi, mid, lo = _split3(v)
    return _dot(m, hi) + _dot(m, mid) + _dot(m, lo)


def _dot_exact_lhs(v, m):
    hi, mid, lo = _split3(v)
    return _dot(hi, m) + _dot(mid, m) + _dot(lo, m)


def _layer_norm(r, g, b):
    mu = jnp.mean(r, axis=-1, keepdims=True)
    d = r - mu
    var = jnp.mean(d * d, axis=-1, keepdims=True)
    return d * lax.rsqrt(var + LN_EPS) * g + b


def _transpose_exact(v, perm):
    tdot = lambda p: lax.dot_general(perm, p, (((1,), (1,)), ((), ())), preferred_element_type=F32)
    hi, mid, lo = _split3(v)
    return tdot(hi) + tdot(mid) + tdot(lo)


def _dt_kernel(x_ref, wdt_ref, brow_ref, alrow_ref, pa_ref, pe_ref, pw_ref, pm_ref,
               rows_ref, acol_ref, colfac_ref, *, n_chunks):
    L = SSD_CHUNK
    hpg = SSD_HEADS_PER_GROUP
    dt_c = _softplus(_dot_t(x_ref[0], wdt_ref[0].astype(BF16)) + brow_ref[...])
    adt_c = dt_c * (-jnp.exp(alrow_ref[...]))
    ri = lax.broadcasted_iota(jnp.int32, (L, L), 0)
    ci = lax.broadcasted_iota(jnp.int32, (L, L), 1)
    lower = (ri >= ci).astype(BF16)
    upper = (ri <= ci).astype(BF16)
    nh = dt_c.shape[1]
    perm = pm_ref[...]
    isb_row = lax.broadcasted_iota(jnp.int32, (1, nh), 1) >= SSD_HEADS
    cums = []
    for c in range(n_chunks):
        a_c = adt_c[c * L:(c + 1) * L]
        cums.append(jnp.where(isb_row, _dot_exact_rhs(upper, a_c), _dot_exact_rhs(lower, a_c)))
    tots = [jnp.broadcast_to(jnp.where(isb_row, cm[0:1], cm[L - 1:L]), cm.shape) for cm in cums]
    cum = jnp.concatenate(cums, axis=0)
    tot = jnp.concatenate(tots, axis=0)
    e_a = jnp.exp(cum)
    w_in = dt_c * jnp.exp(tot - cum)
    e_hi = e_a.astype(BF16)
    e_lo = (e_a - e_hi.astype(F32)).astype(BF16)
    w_hi = w_in.astype(BF16)
    w_lo = (w_in - w_hi.astype(F32)).astype(BF16)
    acol_all = _dot_exact_lhs(cum, pa_ref[...])
    colfac_all = (_dot(e_hi, pe_ref[0]) + _dot(w_hi, pw_ref[0])
                  + _dot(e_lo, pe_ref[1]) + _dot(w_lo, pw_ref[1])).astype(BF16)
    for g in range(SSD_GROUPS):
        acol_ref[0, g] = acol_all[:, g * LANES:(g + 1) * LANES]
        colfac_ref[0, g] = colfac_all[:, g * LANES:(g + 1) * LANES]
    cum_r = _transpose_exact(cum, perm)
    dt_r = _transpose_exact(dt_c, perm)
    row0 = cum_r - jnp.log(dt_r)
    pieces = []
    for g in range(SSD_GROUPS):
        both = jnp.log(dt_r[2 * g * hpg:(2 * g + 1) * hpg] + dt_r[(2 * g + 1) * hpg:(2 * g + 2) * hpg])
        pieces += [both, both]
    row1 = jnp.concatenate(pieces, axis=0)
    for c in range(n_chunks):
        rows_ref[0, c, 0] = row0[:, c * L:(c + 1) * L]
        rows_ref[0, c, 1] = row1[:, c * L:(c + 1) * L]


def _dt_call(xb, w_in_t, layer, row0, brow, alrow, pa, pe, pw, pm):
    bsz, s, d = xb.shape
    nc = s // SSD_CHUNK
    nh = 2 * SSD_HEADS
    const = lambda *shape: pl.BlockSpec(shape, lambda b: (0,) * len(shape))
    return pl.pallas_call(
        functools.partial(_dt_kernel, n_chunks=nc),
        grid=(bsz,),
        in_specs=[pl.BlockSpec((1, s, d), lambda b: (b, 0, 0)),
                  pl.BlockSpec((1, nh, d), lambda b: (layer, row0 // nh, 0)),
                  const(1, nh), const(1, nh),
                  const(nh, SSD_GROUPS * LANES), const(2, nh, SSD_GROUPS * LANES),
                  const(2, nh, SSD_GROUPS * LANES), const(nh, nh)],
        out_specs=[pl.BlockSpec((1, nc, 2, nh, SSD_CHUNK), lambda b: (b, 0, 0, 0, 0)),
                   pl.BlockSpec((1, SSD_GROUPS, s, LANES), lambda b: (b, 0, 0, 0)),
                   pl.BlockSpec((1, SSD_GROUPS, s, LANES), lambda b: (b, 0, 0, 0))],
        out_shape=[jax.ShapeDtypeStruct((bsz, nc, 2, nh, SSD_CHUNK), F32),
                   jax.ShapeDtypeStruct((bsz, SSD_GROUPS, s, LANES), F32),
                   jax.ShapeDtypeStruct((bsz, SSD_GROUPS, s, LANES), BF16)],
        compiler_params=_cparams("parallel"),
        name="dt",
    )(xb, w_in_t, brow, alrow, pa, pe, pw, pm)


def _pool_kernel(x_ref, w_ref, wmap_ref, scale_ref, o_ref, u_ref, ub_ref, *, seq):
    g = pl.program_id(1)
    half = jnp.left_shift(1, g)
    u = _dot_t(x_ref[0], w_ref[0].astype(BF16))
    u_ref[...] = u
    ub_ref[...] = u.astype(BF16)
    blk = min(POOL_BLOCK, seq)
    win = min(seq, blk + 2 * POOL_HALO)
    for r in range(seq // blk):
        start = max(0, min(r * blk - POOL_HALO, seq - win))
        t = r * blk + lax.broadcasted_iota(jnp.int32, (blk, win), 0)
        j = start + lax.broadcasted_iota(jnp.int32, (blk, win), 1)
        band = ((j >= t - half) & (j <= t + half - 1)).astype(BF16)
        ssum = _dot(band, ub_ref[start:start + win, :])
        tq = r * blk + lax.broadcasted_iota(jnp.int32, ssum.shape, 0)
        cnt = jnp.minimum(tq + half - 1, seq - 1) - jnp.maximum(tq - half, 0) + 1
        pooled = ssum / cnt.astype(F32) - u_ref[r * blk:(r + 1) * blk, :]
        out = _dot(pooled.astype(BF16), wmap_ref[0, 0].astype(BF16)) * scale_ref[...]
        o_ref[0, r * blk:(r + 1) * blk, :] = out.astype(o_ref.dtype)


def _pool_call(xb, w_in_t, layer, wmap, scale):
    bsz, s, d = xb.shape
    pg = wmap.shape[-1]
    return pl.pallas_call(
        functools.partial(_pool_kernel, seq=s),
        grid=(bsz, N_POOL_GROUPS),
        in_specs=[pl.BlockSpec((1, s, d), lambda b, g: (b, 0, 0)),
                  pl.BlockSpec((1, pg, d), lambda b, g: (layer, g, 0)),
                  pl.BlockSpec((1, 1, pg, pg), lambda b, g: (layer, g, 0, 0)),
                  pl.BlockSpec((1, pg), lambda b, g: (0, g))],
        out_specs=pl.BlockSpec((1, s, pg), lambda b, g: (b, 0, g)),
        out_shape=jax.ShapeDtypeStruct((bsz, s, N_POOL_GROUPS * pg), BF16),
        scratch_shapes=[pltpu.VMEM((s, pg), F32), pltpu.VMEM((s, pg), BF16)],
        compiler_params=_cparams("parallel", "arbitrary"),
        name="pool",
    )(xb, w_in_t, wmap, scale)


def _act_mm_kernel(x_ref, w_ref, o_ref, *, silu, sub_rows):
    w = w_ref[0].astype(BF16)
    for r in range(x_ref.shape[1] // sub_rows):
        rows = slice(r * sub_rows, (r + 1) * sub_rows)
        h = _dot_t(x_ref[0, rows, :], w)
        sg = _sigmoid(h)
        o_ref[0, rows, :] = (sg * h if silu else sg).astype(o_ref.dtype)


def _act_mm_call(xb, w_t, layer, col0, n_cols, silu, tn, name):
    bsz, s, d = xb.shape
    return pl.pallas_call(
        functools.partial(_act_mm_kernel, silu=silu, sub_rows=min(ROW_SUB_BLOCK, s)),
        grid=(bsz, n_cols // tn),
        in_specs=[pl.BlockSpec((1, s, d), lambda i, j: (i, 0, 0)),
                  pl.BlockSpec((1, tn, d), lambda i, j: (layer, col0 // tn + j, 0))],
        out_specs=pl.BlockSpec((1, s, tn), lambda i, j: (i, 0, j)),
        out_shape=jax.ShapeDtypeStruct((bsz, s, n_cols), BF16),
        compiler_params=_cparams("parallel", "arbitrary"),
        name=name,
    )(xb, w_t)


def _fill_padded(hp_ref, x_ref, w, seq, mm_rows, blocks, transposed_w=False):
    if 0 in blocks:
        zeros = jnp.zeros((CONV_PAD_ROWS, LANES), F32)
        for sl in range(hp_ref.shape[0]):
            hp_ref[sl, 0:CONV_PAD_ROWS, :] = zeros
            hp_ref[sl, CONV_PAD_ROWS + seq:2 * CONV_PAD_ROWS + seq, :] = zeros
    for r in blocks:
        res = (_dot_t if transposed_w else _dot)(x_ref[0, r * mm_rows:(r + 1) * mm_rows, :], w)
        for sl in range(hp_ref.shape[0]):
            hp_ref[sl, CONV_PAD_ROWS + r * mm_rows:CONV_PAD_ROWS + (r + 1) * mm_rows, :] = (
                res[:, sl * LANES:(sl + 1) * LANES])


def _conv_phases(hp_ref, sl, cw_ref, b_ref, row0, rows, taps, lanes):
    pad = taps // 2
    n = rows // CONV_PHASES
    shifted = {m: hp_ref[sl, pl.ds(CONV_PAD_ROWS + row0 + m, n, stride=CONV_PHASES), :]
               for m in range(-pad, CONV_PHASES + pad)}
    outs = []
    for j in range(CONV_PHASES):
        acc = b_ref[:, lanes]
        for k in range(taps):
            acc = acc + cw_ref[k:k + 1, lanes] * shifted[j + k - pad]
        outs.append(acc)
    return outs


def _store_phases(st_ref, sl, row0, rows, phases):
    for j, v in enumerate(phases):
        st_ref[sl, pl.ds(row0 + j, rows // CONV_PHASES, stride=CONV_PHASES), :] = v


def _xbc_kernel(x_ref, w_ref, cw_ref, b_ref, o_ref, *scratch, seq, taps, sub_cols):
    n_sub = len(scratch) // 2
    slabs = sub_cols // LANES
    mm_rows = min(512, seq)
    rows = min(256, seq)
    n_mm = seq // mm_rows
    ws = [w_ref[0, q * sub_cols:(q + 1) * sub_cols, :].astype(BF16) for q in range(n_sub)]

    def conv_block(q, r):
        hp_ref, st_ref = scratch[2 * q], scratch[2 * q + 1]
        for sl in range(slabs):
            gsl = q * slabs + sl
            lanes = slice(gsl * LANES, (gsl + 1) * LANES)
            phases = _conv_phases(hp_ref, sl, cw_ref, b_ref, r * rows, rows, taps, lanes)
            _store_phases(st_ref, sl, r * rows, rows, [v * _sigmoid(v) for v in phases])
            o_ref[0, r * rows:(r + 1) * rows, lanes] = (
                st_ref[sl, r * rows:(r + 1) * rows, :].astype(o_ref.dtype))

    _fill_padded(scratch[0], x_ref, ws[0], seq, mm_rows, range(n_mm), transposed_w=True)
    for q in range(n_sub):
        for m in range(n_mm):
            if q + 1 < n_sub:
                _fill_padded(scratch[2 * q + 2], x_ref, ws[q + 1], seq, mm_rows, [m], transposed_w=True)
            for r in range(m * mm_rows // rows, (m + 1) * mm_rows // rows):
                conv_block(q, r)


def _xbc_call(xb, w_in_t, layer, col0, conv_w, conv_b, tn):
    bsz, s, d = xb.shape
    n = conv_w.shape[1]
    taps = conv_w.shape[0]
    sub_cols = 2 * LANES
    sub_scratch = [pltpu.VMEM((sub_cols // LANES, s + 2 * CONV_PAD_ROWS, LANES), F32),
                   pltpu.VMEM((sub_cols // LANES, s, LANES), F32)]
    return pl.pallas_call(
        functools.partial(_xbc_kernel, seq=s, taps=taps, sub_cols=sub_cols),
        grid=(bsz, n // tn),
        in_specs=[pl.BlockSpec((1, s, d), lambda b, j: (b, 0, 0)),
                  pl.BlockSpec((1, tn, d), lambda b, j: (layer, col0 // tn + j, 0)),
                  pl.BlockSpec((taps, tn), lambda b, j: (0, j)),
                  pl.BlockSpec((1, tn), lambda b, j: (0, j))],
        out_specs=pl.BlockSpec((1, s, tn), lambda b, j: (b, 0, j)),
        out_shape=jax.ShapeDtypeStruct((bsz, s, n), BF16),
        scratch_shapes=sub_scratch * (tn // sub_cols),
        compiler_params=_cparams("parallel", "arbitrary"),
        name="xbc",
    )(xb, w_in_t, conv_w, conv_b)


def _ssd_kernel(x_ref, b_ref, c_ref, zs_ref, rows_ref, acol_ref, colfac_ref, e_ref,
                dskip_ref, gain_ref, o_ref, y_ref, xwf_ref, xwb_ref, scf_ref, scb_ref,
                etf_ref, etb_ref, sf_ref, sb_ref, *, n_chunks):
    L = SSD_CHUNK
    HP = HEAD_GROUP_LANES
    HPG = SSD_HEADS_PER_GROUP
    li = lax.broadcasted_iota(jnp.int32, (L, L), 0)
    si = lax.broadcasted_iota(jnp.int32, (L, L), 1)
    lower = si < li
    upper = si > li
    lane = lax.broadcasted_iota(jnp.int32, (L, 2 * SSD_HEAD_DIM), 1)
    first_head = lane < SSD_HEAD_DIM

    def chunk_local(c):
        rc = pl.ds(pl.multiple_of(c * L, L), L)
        xc = x_ref[0, rc, :]
        bc = b_ref[0, rc, :]
        cc = c_ref[0, rc, :]
        scores = lax.dot_general(cc, bc, (((1,), (1,)), ((), ())), preferred_element_type=F32)
        arow = rows_ref[0, c, 0]
        drow = rows_ref[0, c, 1]
        acol = acol_ref[0, 0, rc, :]
        ypairs = []
        for p in range(HPG // 2):
            dmats = []
            for h in (2 * p, 2 * p + 1):
                hb = HPG + h
                seg = jnp.where(lower, acol[:, h:h + 1] - arow[h:h + 1, :],
                                jnp.where(upper, acol[:, hb:hb + 1] - arow[hb:hb + 1, :],
                                          drow[h:h + 1, :]))
                dmats.append((scores * jnp.exp(seg)).astype(BF16))
            dpair = jnp.concatenate(dmats, axis=1)
            xpair = xc[:, p * 2 * SSD_HEAD_DIM:(p + 1) * 2 * SSD_HEAD_DIM]
            zero = jnp.zeros_like(xpair)
            xbd = jnp.concatenate([jnp.where(first_head, xpair, zero),
                                   jnp.where(first_head, zero, xpair)], axis=0)
            ypairs.append(_dot(dpair, xbd))
        y_ref[rc, :] = jnp.concatenate(ypairs, axis=1)

        fac = _dot(colfac_ref[0, 0, rc, :], e_ref[...])
        xf = xc.astype(F32)
        scf_ref[rc, :] = fac[:, 0 * HP:1 * HP]
        xwf_ref[rc, :] = (xf * fac[:, 1 * HP:2 * HP]).astype(BF16)
        scb_ref[rc, :] = fac[:, 2 * HP:3 * HP]
        xwb_ref[rc, :] = (xf * fac[:, 3 * HP:4 * HP]).astype(BF16)
        etf_ref[c] = jnp.broadcast_to(fac[L - 1:L, 0 * HP:1 * HP], (SUBLANES, HP))
        etb_ref[c] = jnp.broadcast_to(fac[0:1, 2 * HP:3 * HP], (SUBLANES, HP))

    local_unroll = LOCAL_UNROLL if n_chunks % LOCAL_UNROLL == 0 else 1

    def local_step(i, carry):
        for u in range(local_unroll):
            chunk_local(i * local_unroll + u)
        return carry

    lax.fori_loop(0, n_chunks // local_unroll, local_step, 0)

    sf_ref[...] = jnp.zeros(sf_ref.shape, F32)
    sb_ref[...] = jnp.zeros(sb_ref.shape, F32)

    def scan_one(c, s_ref, xw_ref, sc_ref, et_ref):
        rc = pl.ds(pl.multiple_of(c * L, L), L)
        s_old = s_ref[...]
        y_ref[rc, :] += _dot(c_ref[0, rc, :], s_old.astype(BF16)) * sc_ref[rc, :]
        upd = lax.dot_general(b_ref[0, rc, :], xw_ref[rc, :], (((0,), (0,)), ((), ())),
                              preferred_element_type=F32)
        s_ref[...] = s_old * et_ref[c][0:1, :] + upd

    def scan_step(i, carry):
        scan_one(i, sf_ref, xwf_ref, scf_ref, etf_ref)
        scan_one(n_chunks - 1 - i, sb_ref, xwb_ref, scb_ref, etb_ref)
        return carry

    lax.fori_loop(0, n_chunks, scan_step, 0, unroll=SCAN_UNROLL if n_chunks % SCAN_UNROLL == 0 else 1)

    rows = 256 if (n_chunks * L) % 256 == 0 else L
    for r in range(n_chunks * L // rows):
        sl = slice(r * rows, (r + 1) * rows)
        y = y_ref[sl, :] + dskip_ref[...] * x_ref[0, sl, :].astype(F32)
        v = y * zs_ref[0, sl, :].astype(F32)
        v = v * lax.rsqrt(jnp.mean(v * v, axis=-1, keepdims=True) + RMS_EPS)
        o_ref[0, sl, :] = (v * gain_ref[...]).astype(o_ref.dtype)


def _ssd_call(xbc, zs, rows, acol, colfac, e_all, dskip, gain):
    bsz, s, _ = xbc.shape
    nc = s // SSD_CHUNK
    HP = HEAD_GROUP_LANES
    n_x_blocks = SSD_HEADS * SSD_HEAD_DIM // SSD_STATE
    return pl.pallas_call(
        functools.partial(_ssd_kernel, n_chunks=nc),
        grid=(bsz, SSD_GROUPS),
        in_specs=[pl.BlockSpec((1, s, HP), lambda b, g: (b, 0, g)),
                  pl.BlockSpec((1, s, SSD_STATE), lambda b, g: (b, 0, n_x_blocks + g)),
                  pl.BlockSpec((1, s, SSD_STATE), lambda b, g: (b, 0, n_x_blocks + SSD_GROUPS + g)),
                  pl.BlockSpec((1, s, HP), lambda b, g: (b, 0, g)),
                  pl.BlockSpec((1, nc, 2, DIRS_X_HEADS, SSD_CHUNK), lambda b, g: (b, 0, 0, g, 0)),
                  pl.BlockSpec((1, 1, s, LANES), lambda b, g: (b, g, 0, 0)),
                  pl.BlockSpec((1, 1, s, LANES), lambda b, g: (b, g, 0, 0)),
                  pl.BlockSpec((LANES, 4 * HP), lambda b, g: (0, 0)),
                  pl.BlockSpec((1, HP), lambda b, g: (0, g)),
                  pl.BlockSpec((1, HP), lambda b, g: (0, g))],
        out_specs=pl.BlockSpec((1, s, HP), lambda b, g: (b, 0, g)),
        out_shape=jax.ShapeDtypeStruct((bsz, s, SSD_GROUPS * HP), BF16),
        scratch_shapes=[pltpu.VMEM((s, HP), F32),
                        pltpu.VMEM((s, HP), BF16), pltpu.VMEM((s, HP), BF16),
                        pltpu.VMEM((s, HP), F32), pltpu.VMEM((s, HP), F32),
                        pltpu.VMEM((nc, SUBLANES, HP), F32), pltpu.VMEM((nc, SUBLANES, HP), F32),
                        pltpu.VMEM((SSD_STATE, HP), F32), pltpu.VMEM((SSD_STATE, HP), F32)],
        compiler_params=_cparams("parallel", "arbitrary"),
        name="ssd",
    )(xbc, xbc, xbc, zs, rows, acol, colfac, e_all, dskip, gain)


def _merge_kernel(yn_ref, pool_ref, g0_ref, g1_ref, x_ref, wsp32_ref, wout32_ref, lng_ref, lnb_ref,
                  of_ref, ob_ref, wsp_ref, wout_ref, *, alpha, sub_rows):
    @pl.when(pl.program_id(0) == 0)
    def _():
        wsp_ref[...] = wsp32_ref[0].astype(BF16)
        wout_ref[...] = wout32_ref[0].astype(BF16)

    for r in range(x_ref.shape[0] // sub_rows):
        rows = slice(r * sub_rows, (r + 1) * sub_rows)
        ssd_out = _dot(yn_ref[rows, :], wsp_ref[...])
        merged = (g0_ref[rows, :].astype(F32) * pool_ref[rows, :].astype(F32)
                  + g1_ref[rows, :].astype(F32) * ssd_out)
        mix = _dot(merged.astype(BF16), wout_ref[...])
        out = _layer_norm(alpha * x_ref[rows, :] + mix, lng_ref[...], lnb_ref[...])
        of_ref[rows, :] = out
        ob_ref[rows, :] = out.astype(BF16)


def _merge_call(yn, pool_out, gates, x, wsp, wout, layer, lng, lnb, alpha, tm):
    t, d = x.shape
    inner = yn.shape[1]
    row = lambda i: (i, 0)
    const = lambda i: (0, 0)
    resident = lambda shape: pl.BlockSpec(shape, lambda i: (layer, 0, 0), pipeline_mode=pl.Buffered(1))
    return pl.pallas_call(
        functools.partial(_merge_kernel, alpha=alpha, sub_rows=min(ROW_SUB_BLOCK, tm)),
        grid=(t // tm,),
        in_specs=[pl.BlockSpec((tm, inner), row),
                  pl.BlockSpec((tm, d), row),
                  pl.BlockSpec((tm, d), lambda i: (i, 0)),
                  pl.BlockSpec((tm, d), lambda i: (i, 1)),
                  pl.BlockSpec((tm, d), row),
                  resident((1, inner, d)),
                  resident((1, d, d)),
                  pl.BlockSpec((1, d), const),
                  pl.BlockSpec((1, d), const)],
        out_specs=[pl.BlockSpec((tm, d), row), pl.BlockSpec((tm, d), row)],
        out_shape=[jax.ShapeDtypeStruct((t, d), F32), jax.ShapeDtypeStruct((t, d), BF16)],
        scratch_shapes=[pltpu.VMEM((inner, d), BF16), pltpu.VMEM((d, d), BF16)],
        compiler_params=_cparams("arbitrary"),
        name="merge",
    )(yn, pool_out, gates, gates, x, wsp, wout, lng, lnb)


def _ffn_up_kernel(x_ref, wg_ref, wv_ref, cwg_ref, cwv_ref, bg_ref, bv_ref, o_ref, hg_ref, hv_ref,
                   st_ref, *, seq, taps):
    mm_rows = min(512, seq)
    _fill_padded(hg_ref, x_ref, wg_ref[0].astype(BF16), seq, mm_rows, range(seq // mm_rows))
    _fill_padded(hv_ref, x_ref, wv_ref[0].astype(BF16), seq, mm_rows, range(seq // mm_rows))
    rows = min(256, seq)
    for r in range(seq // rows):
        for sl in range(hg_ref.shape[0]):
            lanes = slice(sl * LANES, (sl + 1) * LANES)
            gates = _conv_phases(hg_ref, sl, cwg_ref, bg_ref, r * rows, rows, taps, lanes)
            vals = _conv_phases(hv_ref, sl, cwv_ref, bv_ref, r * rows, rows, taps, lanes)
            acts = [0.5 * g * (1.0 + lax.erf(g * np.float32(2.0 ** -0.5))) * v
                    for g, v in zip(gates, vals)]
            _store_phases(st_ref, sl, r * rows, rows, acts)
            o_ref[0, r * rows:(r + 1) * rows, sl * LANES:(sl + 1) * LANES] = (
                st_ref[sl, r * rows:(r + 1) * rows, :].astype(o_ref.dtype))


def _ffn_up_call(xb, w_up, layer, conv_w, conv_b, tn):
    bsz, s, d = xb.shape
    dff = w_up.shape[2] // 2
    nt = dff // tn
    taps = conv_w.shape[0]
    return pl.pallas_call(
        functools.partial(_ffn_up_kernel, seq=s, taps=taps),
        grid=(bsz, nt),
        in_specs=[pl.BlockSpec((1, s, d), lambda b, j: (b, 0, 0)),
                  pl.BlockSpec((1, d, tn), lambda b, j: (layer, 0, j)),
                  pl.BlockSpec((1, d, tn), lambda b, j: (layer, 0, nt + j)),
                  pl.BlockSpec((taps, tn), lambda b, j: (0, j)),
                  pl.BlockSpec((taps, tn), lambda b, j: (0, nt + j)),
                  pl.BlockSpec((1, tn), lambda b, j: (0, j)),
                  pl.BlockSpec((1, tn), lambda b, j: (0, nt + j))],
        out_specs=pl.BlockSpec((1, s, tn), lambda b, j: (b, 0, j)),
        out_shape=jax.ShapeDtypeStruct((bsz, s, dff), BF16),
        scratch_shapes=[pltpu.VMEM((tn // LANES, s + 2 * CONV_PAD_ROWS, LANES), F32),
                        pltpu.VMEM((tn // LANES, s + 2 * CONV_PAD_ROWS, LANES), F32),
                        pltpu.VMEM((tn // LANES, s, LANES), F32)],
        compiler_params=_cparams("parallel", "arbitrary"),
        name="ffn_up",
    )(xb, w_up, w_up, conv_w, conv_w, conv_b, conv_b)


def _ffn_down_kernel(a_ref, x_ref, w32_ref, lng_ref, lnb_ref, of_ref, ob_ref, w_ref, *, alpha, sub_rows):
    @pl.when(pl.program_id(0) == 0)
    def _():
        w_ref[...] = w32_ref[0].astype(BF16)

    for r in range(x_ref.shape[0] // sub_rows):
        rows = slice(r * sub_rows, (r + 1) * sub_rows)
        out = _layer_norm(alpha * x_ref[rows, :] + _dot(a_ref[rows, :], w_ref[...]),
                          lng_ref[...], lnb_ref[...])
        of_ref[rows, :] = out
        ob_ref[rows, :] = out.astype(BF16)


def _ffn_down_call(act, x, w_down, layer, lng, lnb, alpha, tm):
    t, d = x.shape
    dff = act.shape[1]
    row = lambda i: (i, 0)
    const = lambda i: (0, 0)
    return pl.pallas_call(
        functools.partial(_ffn_down_kernel, alpha=alpha, sub_rows=min(ROW_SUB_BLOCK, tm)),
        grid=(t // tm,),
        in_specs=[pl.BlockSpec((tm, dff), row), pl.BlockSpec((tm, d), row),
                  pl.BlockSpec((1, dff, d), lambda i: (layer, 0, 0), pipeline_mode=pl.Buffered(1)),
                  pl.BlockSpec((1, d), const), pl.BlockSpec((1, d), const)],
        out_specs=[pl.BlockSpec((tm, d), row), pl.BlockSpec((tm, d), row)],
        out_shape=[jax.ShapeDtypeStruct((t, d), F32), jax.ShapeDtypeStruct((t, d), BF16)],
        scratch_shapes=[pltpu.VMEM((dff, d), BF16)],
        compiler_params=_cparams("arbitrary"),
        name="ffn_down",
    )(act, x, w_down, lng, lnb)


def _placement_constants():
    nh = 2 * SSD_HEADS
    pa = np.zeros((nh, SSD_GROUPS * LANES), np.float32)
    pe = np.zeros((2, nh, SSD_GROUPS * LANES), np.float32)
    pw = np.zeros((2, nh, SSD_GROUPS * LANES), np.float32)
    pm = np.zeros((nh, nh), np.float32)
    for g in range(SSD_GROUPS):
        for j in range(DIRS_X_HEADS):
            direction, head = divmod(j, SSD_HEADS_PER_GROUP)
            col = direction * SSD_HEADS + g * SSD_HEADS_PER_GROUP + head
            pm[g * DIRS_X_HEADS + j, col] = 1.0
            lane0 = g * LANES
            pa[col, lane0 + j] = 1.0
            for part in range(2):
                pe[part, col, lane0 + part * 2 * DIRS_X_HEADS + j] = 1.0
                pw[part, col, lane0 + part * 2 * DIRS_X_HEADS + DIRS_X_HEADS + j] = 1.0
    HP = HEAD_GROUP_LANES
    ef = np.zeros((LANES, 2 * HP), np.float32)
    eb = np.zeros((LANES, 2 * HP), np.float32)
    for part in range(2):
        base = part * 2 * DIRS_X_HEADS
        for h in range(SSD_HEADS_PER_GROUP):
            cols = slice(h * SSD_HEAD_DIM, (h + 1) * SSD_HEAD_DIM)
            cols_w = slice(HP + h * SSD_HEAD_DIM, HP + (h + 1) * SSD_HEAD_DIM)
            ef[base + h, cols] = 1.0
            eb[base + SSD_HEADS_PER_GROUP + h, cols] = 1.0
            ef[base + DIRS_X_HEADS + h, cols_w] = 1.0
            eb[base + DIRS_X_HEADS + SSD_HEADS_PER_GROUP + h, cols_w] = 1.0
    as_bf16 = lambda a: jnp.asarray(a, BF16)
    return as_bf16(pa), as_bf16(pe), as_bf16(pw), as_bf16(pm), as_bf16(np.concatenate([ef, eb], axis=1))


def kernel(x, w_in, pool_w, pool_scale, ssd_conv_w, ssd_conv_b, a_log, dt_bias, d_skip, ssd_norm_g,
           w_ssd_proj, w_out, ln1_g, ln1_b, w_up, ffn_conv_w, ffn_conv_b, w_down, ln2_g, ln2_b):
    bsz, s, d = x.shape
    depth = w_in.shape[0]
    t = bsz * s
    alpha = float((2 * depth) ** 0.25)
    pool_width = pool_w.shape[1] * pool_w.shape[2]
    inner = w_ssd_proj.shape[1]
    conv_ch = ssd_conv_w.shape[2]
    n_dt = 2 * SSD_HEADS
    c0, c1, c2, c3 = pool_width, pool_width + inner, pool_width + inner + conv_ch, \
        pool_width + inner + conv_ch + n_dt
    tm = min(1024, t)
    pa, pe, pw, pm, e_all = _placement_constants()

    w_in_t = jnp.swapaxes(w_in, 1, 2)
    xf = x.reshape(t, d)
    xb3 = x.astype(BF16)
    for i in range(depth):
        w_gate_t = w_in_t[i, c3:][None]
        rows, acol, colfac = _dt_call(xb3, w_in_t, i, c2, dt_bias[i].reshape(1, -1),
                                      a_log[i].reshape(1, -1), pa, pe, pw, pm)
        pool_out = _pool_call(xb3, w_in_t, i, pool_w, pool_scale[i].reshape(1, -1))
        zs = _act_mm_call(xb3, w_in_t, i, c0, inner, True, 512, "z")
        gates = _act_mm_call(xb3, w_gate_t, 0, 0, 2 * d, False, 512, "gates").reshape(t, 2 * d)
        xbc = _xbc_call(xb3, w_in_t, i, c1, ssd_conv_w[i], ssd_conv_b[i].reshape(1, -1), 512)
        yn = _ssd_call(xbc, zs, rows, acol, colfac, e_all,
                       jnp.repeat(d_skip[i], SSD_HEAD_DIM).reshape(1, -1),
                       ssd_norm_g[i].reshape(1, -1))
        xf, xb = _merge_call(yn.reshape(t, inner), pool_out.reshape(t, d), gates, xf,
                             w_ssd_proj, w_out, i, ln1_g[i].reshape(1, -1), ln1_b[i].reshape(1, -1),
                             alpha, min(512, t))
        act = _ffn_up_call(xb.reshape(bsz, s, d), w_up, i, ffn_conv_w[i],
                           ffn_conv_b[i].reshape(1, -1), 256)
        xf, xb = _ffn_down_call(act.reshape(t, -1), xf, w_down, i,
                                ln2_g[i].reshape(1, -1), ln2_b[i].reshape(1, -1), alpha, tm)
        xb3 = xb.reshape(bsz, s, d)
    return xf.reshape(bsz, s, d)
```

```python
import functools

import numpy as np
import jax
import jax.numpy as jnp
from jax import lax
from jax.experimental import pallas as pl
from jax.experimental.pallas import tpu as pltpu

F32 = jnp.float32
BF16 = jnp.bfloat16

POOL_WINDOWS = (2, 4, 8, 16)
N_POOL_GROUPS = 4
SSD_HEAD_DIM = 64
SSD_GROUPS = 4
SSD_HEADS_PER_GROUP = 8
SSD_HEADS = SSD_GROUPS * SSD_HEADS_PER_GROUP
SSD_STATE = 128
SSD_CHUNK = 128
LN_EPS = 1e-5
RMS_EPS = 1e-5

LANES = 128
SUBLANES = 8
VMEM_LIMIT_BYTES = 56 * 1024 * 1024

CONV_PAD_ROWS = SUBLANES
CONV_PHASES = 4
POOL_BLOCK = 256
POOL_HALO = 16
HEAD_GROUP_LANES = SSD_HEADS_PER_GROUP * SSD_HEAD_DIM
DIRS_X_HEADS = 2 * SSD_HEADS_PER_GROUP
LOCAL_UNROLL = 16
SCAN_UNROLL = 8
ROW_SUB_BLOCK = 256

def _cparams(*sem):
    return pltpu.CompilerParams(dimension_semantics=sem, vmem_limit_bytes=VMEM_LIMIT_BYTES)


def _sigmoid(v):
    return 1.0 / (1.0 + jnp.exp(-v))


def _silu_tanh(v):
    h = 0.5 * v
    return h + h * jnp.tanh(h)


def _softplus(v):
    return jnp.maximum(v, 0.0) + jnp.log1p(jnp.exp(-jnp.abs(v)))


def _dot(a, b):
    return jnp.dot(a, b, preferred_element_type=F32)


def _dot_t(a, b_t):
    return lax.dot_general(a, b_t, (((1,), (1,)), ((), ())), preferred_element_type=F32)


def _split3(v):
    hi = v.astype(BF16)
    r1 = v - hi.astype(F32)
    mid = r1.astype(BF16)
    lo = (r1 - mid.astype(F32)).astype(BF16)
    return hi, mid, lo


def _dot_exact_rhs(m, v):
    hi, mid, lo = _split3(v)
    return _dot(m, hi) + _dot(m, mid) + _dot(m, lo)


def _dot_exact_lhs(v, m):
    hi, mid, lo = _split3(v)
    return _dot(hi, m) + _dot(mid, m) + _dot(lo, m)


def _layer_norm(r, g, b):
    mu = jnp.mean(r, axis=-1, keepdims=True)
    d = r - mu
    var = jnp.mean(d * d, axis=-1, keepdims=True)
    return d * lax.rsqrt(var + LN_EPS) * g + b


def _transpose_exact(v, perm):
    tdot = lambda p: lax.dot_general(perm, p, (((1,), (1,)), ((), ())), preferred_element_type=F32)
    hi, mid, lo = _split3(v)
    return tdot(hi) + tdot(mid) + tdot(lo)


def _dt_kernel(x_ref, wdt_ref, brow_ref, alrow_ref, pa_ref, pe_ref, pw_ref, pm_ref,
               rows_ref, acol_ref, colfac_ref, *, n_chunks):
    L = SSD_CHUNK
    hpg = SSD_HEADS_PER_GROUP
    dt_c = _softplus(_dot_t(x_ref[0], wdt_ref[0].astype(BF16)) + brow_ref[...])
    adt_c = dt_c * (-jnp.exp(alrow_ref[...]))
    ri = lax.broadcasted_iota(jnp.int32, (L, L), 0)
    ci = lax.broadcasted_iota(jnp.int32, (L, L), 1)
    lower = (ri >= ci).astype(BF16)
    upper = (ri <= ci).astype(BF16)
    nh = dt_c.shape[1]
    perm = pm_ref[...]
    isb_row = lax.broadcasted_iota(jnp.int32, (1, nh), 1) >= SSD_HEADS
    cums = []
    for c in range(n_chunks):
        a_c = adt_c[c * L:(c + 1) * L]
        cums.append(jnp.where(isb_row, _dot_exact_rhs(upper, a_c), _dot_exact_rhs(lower, a_c)))
    tots = [jnp.broadcast_to(jnp.where(isb_row, cm[0:1], cm[L - 1:L]), cm.shape) for cm in cums]
    cum = jnp.concatenate(cums, axis=0)
    tot = jnp.concatenate(tots, axis=0)
    e_a = jnp.exp(cum)
    w_in = dt_c * jnp.exp(tot - cum)
    e_hi = e_a.astype(BF16)
    e_lo = (e_a - e_hi.astype(F32)).astype(BF16)
    w_hi = w_in.astype(BF16)
    w_lo = (w_in - w_hi.astype(F32)).astype(BF16)
    acol_all = _dot_exact_lhs(cum, pa_ref[...])
    colfac_all = (_dot(e_hi, pe_ref[0]) + _dot(w_hi, pw_ref[0])
                  + _dot(e_lo, pe_ref[1]) + _dot(w_lo, pw_ref[1])).astype(BF16)
    for g in range(SSD_GROUPS):
        acol_ref[0, g] = acol_all[:, g * LANES:(g + 1) * LANES]
        colfac_ref[0, g] = colfac_all[:, g * LANES:(g + 1) * LANES]
    cum_r = _transpose_exact(cum, perm)
    dt_r = _transpose_exact(dt_c, perm)
    row0 = cum_r - jnp.log(dt_r)
    pieces = []
    for g in range(SSD_GROUPS):
        both = jnp.log(dt_r[2 * g * hpg:(2 * g + 1) * hpg] + dt_r[(2 * g + 1) * hpg:(2 * g + 2) * hpg])
        pieces += [both, both]
    row1 = jnp.concatenate(pieces, axis=0)
    for c in range(n_chunks):
        rows_ref[0, c, 0] = row0[:, c * L:(c + 1) * L]
        rows_ref[0, c, 1] = row1[:, c * L:(c + 1) * L]


def _dt_call(xb, w_in_t, layer, row0, brow, alrow, pa, pe, pw, pm):
    bsz, s, d = xb.shape
    nc = s // SSD_CHUNK
    nh = 2 * SSD_HEADS
    const = lambda *shape: pl.BlockSpec(shape, lambda b: (0,) * len(shape))
    return pl.pallas_call(
        functools.partial(_dt_kernel, n_chunks=nc),
        grid=(bsz,),
        in_specs=[pl.BlockSpec((1, s, d), lambda b: (b, 0, 0)),
                  pl.BlockSpec((1, nh, d), lambda b: (layer, row0 // nh, 0)),
                  const(1, nh), const(1, nh),
                  const(nh, SSD_GROUPS * LANES), const(2, nh, SSD_GROUPS * LANES),
                  const(2, nh, SSD_GROUPS * LANES), const(nh, nh)],
        out_specs=[pl.BlockSpec((1, nc, 2, nh, SSD_CHUNK), lambda b: (b, 0, 0, 0, 0)),
                   pl.BlockSpec((1, SSD_GROUPS, s, LANES), lambda b: (b, 0, 0, 0)),
                   pl.BlockSpec((1, SSD_GROUPS, s, LANES), lambda b: (b, 0, 0, 0))],
        out_shape=[jax.ShapeDtypeStruct((bsz, nc, 2, nh, SSD_CHUNK), F32),
                   jax.ShapeDtypeStruct((bsz, SSD_GROUPS, s, LANES), F32),
                   jax.ShapeDtypeStruct((bsz, SSD_GROUPS, s, LANES), BF16)],
        compiler_params=_cparams("parallel"),
        name="dt",
    )(xb, w_in_t, brow, alrow, pa, pe, pw, pm)


def _pool_kernel(x_ref, w_ref, wmap_ref, scale_ref, o_ref, u_ref, ub_ref, *, seq):
    g = pl.program_id(1)
    half = jnp.left_shift(1, g)
    u = _dot_t(x_ref[0], w_ref[0].astype(BF16))
    u_ref[...] = u
    ub_ref[...] = u.astype(BF16)
    blk = min(POOL_BLOCK, seq)
    win = min(seq, blk + 2 * POOL_HALO)
    for r in range(seq // blk):
        start = max(0, min(r * blk - POOL_HALO, seq - win))
        t = r * blk + lax.broadcasted_iota(jnp.int32, (blk, win), 0)
        j = start + lax.broadcasted_iota(jnp.int32, (blk, win), 1)
        band = ((j >= t - half) & (j <= t + half - 1)).astype(BF16)
        ssum = _dot(band, ub_ref[start:start + win, :])
        tq = r * blk + lax.broadcasted_iota(jnp.int32, ssum.shape, 0)
        cnt = jnp.minimum(tq + half - 1, seq - 1) - jnp.maximum(tq - half, 0) + 1
        pooled = ssum / cnt.astype(F32) - u_ref[r * blk:(r + 1) * blk, :]
        out = _dot(pooled.astype(BF16), wmap_ref[0, 0].astype(BF16)) * scale_ref[...]
        o_ref[0, r * blk:(r + 1) * blk, :] = out.astype(o_ref.dtype)


def _pool_call(xb, w_in_t, layer, wmap, scale):
    bsz, s, d = xb.shape
    pg = wmap.shape[-1]
    return pl.pallas_call(
        functools.partial(_pool_kernel, seq=s),
        grid=(bsz, N_POOL_GROUPS),
        in_specs=[pl.BlockSpec((1, s, d), lambda b, g: (b, 0, 0)),
                  pl.BlockSpec((1, pg, d), lambda b, g: (layer, g, 0)),
                  pl.BlockSpec((1, 1, pg, pg), lambda b, g: (layer, g, 0, 0)),
                  pl.BlockSpec((1, pg), lambda b, g: (0, g))],
        out_specs=pl.BlockSpec((1, s, pg), lambda b, g: (b, 0, g)),
        out_shape=jax.ShapeDtypeStruct((bsz, s, N_POOL_GROUPS * pg), BF16),
        scratch_shapes=[pltpu.VMEM((s, pg), F32), pltpu.VMEM((s, pg), BF16)],
        compiler_params=_cparams("parallel", "arbitrary"),
        name="pool",
    )(xb, w_in_t, wmap, scale)


def _act_mm_kernel(x_ref, w_ref, o_ref, *, silu, sub_rows):
    w = w_ref[0].astype(BF16)
    for r in range(x_ref.shape[1] // sub_rows):
        rows = slice(r * sub_rows, (r + 1) * sub_rows)
        h = _dot_t(x_ref[0, rows, :], w)
        sg = _sigmoid(h)
        o_ref[0, rows, :] = (sg * h if silu else sg).astype(o_ref.dtype)


def _act_mm_call(xb, w_t, layer, col0, n_cols, silu, tn, name):
    bsz, s, d = xb.shape
    return pl.pallas_call(
        functools.partial(_act_mm_kernel, silu=silu, sub_rows=min(ROW_SUB_BLOCK, s)),
        grid=(bsz, n_cols // tn),
        in_specs=[pl.BlockSpec((1, s, d), lambda i, j: (i, 0, 0)),
                  pl.BlockSpec((1, tn, d), lambda i, j: (layer, col0 // tn + j, 0))],
        out_specs=pl.BlockSpec((1, s, tn), lambda i, j: (i, 0, j)),
        out_shape=jax.ShapeDtypeStruct((bsz, s, n_cols), BF16),
        compiler_params=_cparams("parallel", "arbitrary"),
        name=name,
    )(xb, w_t)


def _fill_padded(hp_ref, x_ref, w, seq, mm_rows, blocks, transposed_w=False):
    if 0 in blocks:
        zeros = jnp.zeros((CONV_PAD_ROWS, LANES), F32)
        for sl in range(hp_ref.shape[0]):
            hp_ref[sl, 0:CONV_PAD_ROWS, :] = zeros
            hp_ref[sl, CONV_PAD_ROWS + seq:2 * CONV_PAD_ROWS + seq, :] = zeros
    for r in blocks:
        res = (_dot_t if transposed_w else _dot)(x_ref[0, r * mm_rows:(r + 1) * mm_rows, :], w)
        for sl in range(hp_ref.shape[0]):
            hp_ref[sl, CONV_PAD_ROWS + r * mm_rows:CONV_PAD_ROWS + (r + 1) * mm_rows, :] = (
                res[:, sl * LANES:(sl + 1) * LANES])


def _conv_phases(hp_ref, sl, cw_ref, b_ref, row0, rows, taps, lanes):
    pad = taps // 2
    n = rows // CONV_PHASES
    shifted = {m: hp_ref[sl, pl.ds(CONV_PAD_ROWS + row0 + m, n, stride=CONV_PHASES), :]
               for m in range(-pad, CONV_PHASES + pad)}
    outs = []
    for j in range(CONV_PHASES):
        acc = b_ref[:, lanes]
        for k in range(taps):
            acc = acc + cw_ref[k:k + 1, lanes] * shifted[j + k - pad]
        outs.append(acc)
    return outs


def _store_phases(st_ref, sl, row0, rows, phases):
    for j, v in enumerate(phases):
        st_ref[sl, pl.ds(row0 + j, rows // CONV_PHASES, stride=CONV_PHASES), :] = v


def _xbc_kernel(x_ref, w_ref, cw_ref, b_ref, o_ref, *scratch, seq, taps, sub_cols):
    n_sub = len(scratch) // 2
    slabs = sub_cols // LANES
    mm_rows = min(512, seq)
    rows = min(256, seq)
    n_mm = seq // mm_rows
    ws = [w_ref[0, q * sub_cols:(q + 1) * sub_cols, :].astype(BF16) for q in range(n_sub)]

    def conv_block(q, r):
        hp_ref, st_ref = scratch[2 * q], scratch[2 * q + 1]
        for sl in range(slabs):
            gsl = q * slabs + sl
            lanes = slice(gsl * LANES, (gsl + 1) * LANES)
            phases = _conv_phases(hp_ref, sl, cw_ref, b_ref, r * rows, rows, taps, lanes)
            _store_phases(st_ref, sl, r * rows, rows, [_silu_tanh(v) for v in phases])
            o_ref[0, r * rows:(r + 1) * rows, lanes] = (
                st_ref[sl, r * rows:(r + 1) * rows, :].astype(o_ref.dtype))

    _fill_padded(scratch[0], x_ref, ws[0], seq, mm_rows, range(n_mm), transposed_w=True)
    for q in range(n_sub):
        for m in range(n_mm):
            if q + 1 < n_sub:
                _fill_padded(scratch[2 * q + 2], x_ref, ws[q + 1], seq, mm_rows, [m], transposed_w=True)
            for r in range(m * mm_rows // rows, (m + 1) * mm_rows // rows):
                conv_block(q, r)


def _xbc_call(xb, w_in_t, layer, col0, conv_w, conv_b, tn):
    bsz, s, d = xb.shape
    n = conv_w.shape[1]
    taps = conv_w.shape[0]
    sub_cols = 4 * LANES
    sub_scratch = [pltpu.VMEM((sub_cols // LANES, s + 2 * CONV_PAD_ROWS, LANES), F32),
                   pltpu.VMEM((sub_cols // LANES, s, LANES), F32)]
    return pl.pallas_call(
        functools.partial(_xbc_kernel, seq=s, taps=taps, sub_cols=sub_cols),
        grid=(bsz, n // tn),
        in_specs=[pl.BlockSpec((1, s, d), lambda b, j: (b, 0, 0)),
                  pl.BlockSpec((1, tn, d), lambda b, j: (layer, col0 // tn + j, 0)),
                  pl.BlockSpec((taps, tn), lambda b, j: (0, j)),
                  pl.BlockSpec((1, tn), lambda b, j: (0, j))],
        out_specs=pl.BlockSpec((1, s, tn), lambda b, j: (b, 0, j)),
        out_shape=jax.ShapeDtypeStruct((bsz, s, n), BF16),
        scratch_shapes=sub_scratch * (tn // sub_cols),
        compiler_params=_cparams("parallel", "arbitrary"),
        name="xbc",
    )(xb, w_in_t, conv_w, conv_b)


def _ssd_kernel(x_ref, b_ref, c_ref, zs_ref, rows_ref, acol_ref, colfac_ref, e_ref,
                dskip_ref, gain_ref, o_ref, y_ref, xwf_ref, xwb_ref, scf_ref, scb_ref,
                etf_ref, etb_ref, sf_ref, sb_ref, *, n_chunks):
    L = SSD_CHUNK
    HP = HEAD_GROUP_LANES
    HPG = SSD_HEADS_PER_GROUP
    li = lax.broadcasted_iota(jnp.int32, (L, L), 0)
    si = lax.broadcasted_iota(jnp.int32, (L, L), 1)
    lower = si < li
    upper = si > li
    lane = lax.broadcasted_iota(jnp.int32, (L, 2 * SSD_HEAD_DIM), 1)
    first_head = lane < SSD_HEAD_DIM

    def chunk_local(c):
        rc = pl.ds(pl.multiple_of(c * L, L), L)
        xc = x_ref[0, rc, :]
        bc = b_ref[0, rc, :]
        cc = c_ref[0, rc, :]
        scores = lax.dot_general(cc, bc, (((1,), (1,)), ((), ())), preferred_element_type=F32)
        arow = rows_ref[0, c, 0]
        drow = rows_ref[0, c, 1]
        acol = acol_ref[0, 0, rc, :]
        ypairs = []
        for p in range(HPG // 2):
            dmats = []
            for h in (2 * p, 2 * p + 1):
                hb = HPG + h
                seg = jnp.where(lower, acol[:, h:h + 1] - arow[h:h + 1, :],
                                jnp.where(upper, acol[:, hb:hb + 1] - arow[hb:hb + 1, :],
                                          drow[h:h + 1, :]))
                dmats.append((scores * jnp.exp(seg)).astype(BF16))
            dpair = jnp.concatenate(dmats, axis=1)
            xpair = xc[:, p * 2 * SSD_HEAD_DIM:(p + 1) * 2 * SSD_HEAD_DIM]
            zero = jnp.zeros_like(xpair)
            xbd = jnp.concatenate([jnp.where(first_head, xpair, zero),
                                   jnp.where(first_head, zero, xpair)], axis=0)
            ypairs.append(_dot(dpair, xbd))
        xf = xc.astype(F32)
        y_ref[rc, :] = jnp.concatenate(ypairs, axis=1) + dskip_ref[...] * xf

        fac = _dot(colfac_ref[0, 0, rc, :], e_ref[...])
        scf_ref[rc, :] = fac[:, 0 * HP:1 * HP]
        xwf_ref[rc, :] = (xf * fac[:, 1 * HP:2 * HP]).astype(BF16)
        scb_ref[rc, :] = fac[:, 2 * HP:3 * HP]
        xwb_ref[rc, :] = (xf * fac[:, 3 * HP:4 * HP]).astype(BF16)
        etf_ref[c] = jnp.broadcast_to(fac[L - 1:L, 0 * HP:1 * HP], (SUBLANES, HP))
        etb_ref[c] = jnp.broadcast_to(fac[0:1, 2 * HP:3 * HP], (SUBLANES, HP))

    local_unroll = LOCAL_UNROLL if n_chunks % LOCAL_UNROLL == 0 else 1

    def local_step(i, carry):
        for u in range(local_unroll):
            chunk_local(i * local_unroll + u)
        return carry

    lax.fori_loop(0, n_chunks // local_unroll, local_step, 0)

    sf_ref[...] = jnp.zeros(sf_ref.shape, F32)
    sb_ref[...] = jnp.zeros(sb_ref.shape, F32)

    def scan_one(c, s_ref, xw_ref, sc_ref, et_ref):
        rc = pl.ds(pl.multiple_of(c * L, L), L)
        s_old = s_ref[...]
        y_ref[rc, :] += _dot(c_ref[0, rc, :], s_old.astype(BF16)) * sc_ref[rc, :]
        upd = lax.dot_general(b_ref[0, rc, :], xw_ref[rc, :], (((0,), (0,)), ((), ())),
                              preferred_element_type=F32)
        s_ref[...] = s_old * et_ref[c][0:1, :] + upd

    def scan_step(i, carry):
        scan_one(i, sf_ref, xwf_ref, scf_ref, etf_ref)
        scan_one(n_chunks - 1 - i, sb_ref, xwb_ref, scb_ref, etb_ref)
        return carry

    lax.fori_loop(0, n_chunks, scan_step, 0, unroll=SCAN_UNROLL if n_chunks % SCAN_UNROLL == 0 else 1)

    rows = 256 if (n_chunks * L) % 256 == 0 else L
    for r in range(n_chunks * L // rows):
        sl = slice(r * rows, (r + 1) * rows)
        v = y_ref[sl, :] * zs_ref[0, sl, :].astype(F32)
        v = v * lax.rsqrt(jnp.mean(v * v, axis=-1, keepdims=True) + RMS_EPS)
        o_ref[0, sl, :] = (v * gain_ref[...]).astype(o_ref.dtype)


def _ssd_call(xbc, zs, rows, acol, colfac, e_all, dskip, gain):
    bsz, s, _ = xbc.shape
    nc = s // SSD_CHUNK
    HP = HEAD_GROUP_LANES
    n_x_blocks = SSD_HEADS * SSD_HEAD_DIM // SSD_STATE
    return pl.pallas_call(
        functools.partial(_ssd_kernel, n_chunks=nc),
        grid=(bsz, SSD_GROUPS),
        in_specs=[pl.BlockSpec((1, s, HP), lambda b, g: (b, 0, g)),
                  pl.BlockSpec((1, s, SSD_STATE), lambda b, g: (b, 0, n_x_blocks + g)),
                  pl.BlockSpec((1, s, SSD_STATE), lambda b, g: (b, 0, n_x_blocks + SSD_GROUPS + g)),
                  pl.BlockSpec((1, s, HP), lambda b, g: (b, 0, g)),
                  pl.BlockSpec((1, nc, 2, DIRS_X_HEADS, SSD_CHUNK), lambda b, g: (b, 0, 0, g, 0)),
                  pl.BlockSpec((1, 1, s, LANES), lambda b, g: (b, g, 0, 0)),
                  pl.BlockSpec((1, 1, s, LANES), lambda b, g: (b, g, 0, 0)),
                  pl.BlockSpec((LANES, 4 * HP), lambda b, g: (0, 0)),
                  pl.BlockSpec((1, HP), lambda b, g: (0, g)),
                  pl.BlockSpec((1, HP), lambda b, g: (0, g))],
        out_specs=pl.BlockSpec((1, s, HP), lambda b, g: (b, 0, g)),
        out_shape=jax.ShapeDtypeStruct((bsz, s, SSD_GROUPS * HP), BF16),
        scratch_shapes=[pltpu.VMEM((s, HP), F32),
                        pltpu.VMEM((s, HP), BF16), pltpu.VMEM((s, HP), BF16),
                        pltpu.VMEM((s, HP), F32), pltpu.VMEM((s, HP), F32),
                        pltpu.VMEM((nc, SUBLANES, HP), F32), pltpu.VMEM((nc, SUBLANES, HP), F32),
                        pltpu.VMEM((SSD_STATE, HP), F32), pltpu.VMEM((SSD_STATE, HP), F32)],
        compiler_params=_cparams("parallel", "arbitrary"),
        name="ssd",
    )(xbc, xbc, xbc, zs, rows, acol, colfac, e_all, dskip, gain)


def _merge_kernel(yn_ref, pool_ref, g0_ref, g1_ref, x_ref, wsp32_ref, wout32_ref, lng_ref, lnb_ref,
                  of_ref, ob_ref, wsp_ref, wout_ref, *, alpha, sub_rows):
    @pl.when(pl.program_id(0) == 0)
    def _():
        wsp_ref[...] = wsp32_ref[0].astype(BF16)
        wout_ref[...] = wout32_ref[0].astype(BF16)

    for r in range(x_ref.shape[0] // sub_rows):
        rows = slice(r * sub_rows, (r + 1) * sub_rows)
        ssd_out = _dot(yn_ref[rows, :], wsp_ref[...])
        merged = (g0_ref[rows, :].astype(F32) * pool_ref[rows, :].astype(F32)
                  + g1_ref[rows, :].astype(F32) * ssd_out)
        mix = _dot(merged.astype(BF16), wout_ref[...])
        out = _layer_norm(alpha * x_ref[rows, :] + mix, lng_ref[...], lnb_ref[...])
        of_ref[rows, :] = out
        ob_ref[rows, :] = out.astype(BF16)


def _merge_call(yn, pool_out, gates, x, wsp, wout, layer, lng, lnb, alpha, tm):
    t, d = x.shape
    inner = yn.shape[1]
    row = lambda i: (i, 0)
    const = lambda i: (0, 0)
    resident = lambda shape: pl.BlockSpec(shape, lambda i: (layer, 0, 0), pipeline_mode=pl.Buffered(1))
    return pl.pallas_call(
        functools.partial(_merge_kernel, alpha=alpha, sub_rows=min(ROW_SUB_BLOCK, tm)),
        grid=(t // tm,),
        in_specs=[pl.BlockSpec((tm, inner), row),
                  pl.BlockSpec((tm, d), row),
                  pl.BlockSpec((tm, d), lambda i: (i, 0)),
                  pl.BlockSpec((tm, d), lambda i: (i, 1)),
                  pl.BlockSpec((tm, d), row),
                  resident((1, inner, d)),
                  resident((1, d, d)),
                  pl.BlockSpec((1, d), const),
                  pl.BlockSpec((1, d), const)],
        out_specs=[pl.BlockSpec((tm, d), row), pl.BlockSpec((tm, d), row)],
        out_shape=[jax.ShapeDtypeStruct((t, d), F32), jax.ShapeDtypeStruct((t, d), BF16)],
        scratch_shapes=[pltpu.VMEM((inner, d), BF16), pltpu.VMEM((d, d), BF16)],
        compiler_params=_cparams("arbitrary"),
        name="merge",
    )(yn, pool_out, gates, gates, x, wsp, wout, lng, lnb)


def _ffn_up_kernel(x_ref, wg_ref, wv_ref, cwg_ref, cwv_ref, bg_ref, bv_ref, o_ref, hg_ref, hv_ref,
                   st_ref, *, seq, taps):
    mm_rows = min(512, seq)
    _fill_padded(hg_ref, x_ref, wg_ref[0].astype(BF16), seq, mm_rows, range(seq // mm_rows))
    _fill_padded(hv_ref, x_ref, wv_ref[0].astype(BF16), seq, mm_rows, range(seq // mm_rows))
    rows = min(256, seq)
    half_cwv = 0.5 * cwv_ref[...]
    half_bv = 0.5 * bv_ref[...]
    for r in range(seq // rows):
        for sl in range(hg_ref.shape[0]):
            lanes = slice(sl * LANES, (sl + 1) * LANES)
            gates = _conv_phases(hg_ref, sl, cwg_ref, bg_ref, r * rows, rows, taps, lanes)
            half_vals = _conv_phases(hv_ref, sl, half_cwv, half_bv, r * rows, rows, taps, lanes)
            acts = [g * (1.0 + lax.erf(g * np.float32(2.0 ** -0.5))) * hv
                    for g, hv in zip(gates, half_vals)]
            _store_phases(st_ref, sl, r * rows, rows, acts)
            o_ref[0, r * rows:(r + 1) * rows, sl * LANES:(sl + 1) * LANES] = (
                st_ref[sl, r * rows:(r + 1) * rows, :].astype(o_ref.dtype))


def _ffn_up_call(xb, w_up, layer, conv_w, conv_b, tn):
    bsz, s, d = xb.shape
    dff = w_up.shape[2] // 2
    nt = dff // tn
    taps = conv_w.shape[0]
    return pl.pallas_call(
        functools.partial(_ffn_up_kernel, seq=s, taps=taps),
        grid=(bsz, nt),
        in_specs=[pl.BlockSpec((1, s, d), lambda b, j: (b, 0, 0)),
                  pl.BlockSpec((1, d, tn), lambda b, j: (layer, 0, j)),
                  pl.BlockSpec((1, d, tn), lambda b, j: (layer, 0, nt + j)),
                  pl.BlockSpec((taps, tn), lambda b, j: (0, j)),
                  pl.BlockSpec((taps, tn), lambda b, j: (0, nt + j)),
                  pl.BlockSpec((1, tn), lambda b, j: (0, j)),
                  pl.BlockSpec((1, tn), lambda b, j: (0, nt + j))],
        out_specs=pl.BlockSpec((1, s, tn), lambda b, j: (b, 0, j)),
        out_shape=jax.ShapeDtypeStruct((bsz, s, dff), BF16),
        scratch_shapes=[pltpu.VMEM((tn // LANES, s + 2 * CONV_PAD_ROWS, LANES), F32),
                        pltpu.VMEM((tn // LANES, s + 2 * CONV_PAD_ROWS, LANES), F32),
                        pltpu.VMEM((tn // LANES, s, LANES), F32)],
        compiler_params=_cparams("parallel", "arbitrary"),
        name="ffn_up",
    )(xb, w_up, w_up, conv_w, conv_w, conv_b, conv_b)


def _ffn_down_kernel(a_ref, x_ref, w32_ref, lng_ref, lnb_ref, of_ref, ob_ref, w_ref, *, alpha, sub_rows):
    @pl.when(pl.program_id(0) == 0)
    def _():
        w_ref[...] = w32_ref[0].astype(BF16)

    for r in range(x_ref.shape[0] // sub_rows):
        rows = slice(r * sub_rows, (r + 1) * sub_rows)
        out = _layer_norm(alpha * x_ref[rows, :] + _dot(a_ref[rows, :], w_ref[...]),
                          lng_ref[...], lnb_ref[...])
        of_ref[rows, :] = out
        ob_ref[rows, :] = out.astype(BF16)


def _ffn_down_call(act, x, w_down, layer, lng, lnb, alpha, tm):
    t, d = x.shape
    dff = act.shape[1]
    row = lambda i: (i, 0)
    const = lambda i: (0, 0)
    return pl.pallas_call(
        functools.partial(_ffn_down_kernel, alpha=alpha, sub_rows=min(ROW_SUB_BLOCK, tm)),
        grid=(t // tm,),
        in_specs=[pl.BlockSpec((tm, dff), row), pl.BlockSpec((tm, d), row),
                  pl.BlockSpec((1, dff, d), lambda i: (layer, 0, 0), pipeline_mode=pl.Buffered(1)),
                  pl.BlockSpec((1, d), const), pl.BlockSpec((1, d), const)],
        out_specs=[pl.BlockSpec((tm, d), row), pl.BlockSpec((tm, d), row)],
        out_shape=[jax.ShapeDtypeStruct((t, d), F32), jax.ShapeDtypeStruct((t, d), BF16)],
        scratch_shapes=[pltpu.VMEM((dff, d), BF16)],
        compiler_params=_cparams("arbitrary"),
        name="ffn_down",
    )(act, x, w_down, lng, lnb)


def _placement_constants():
    nh = 2 * SSD_HEADS
    pa = np.zeros((nh, SSD_GROUPS * LANES), np.float32)
    pe = np.zeros((2, nh, SSD_GROUPS * LANES), np.float32)
    pw = np.zeros((2, nh, SSD_GROUPS * LANES), np.float32)
    pm = np.zeros((nh, nh), np.float32)
    for g in range(SSD_GROUPS):
        for j in range(DIRS_X_HEADS):
            direction, head = divmod(j, SSD_HEADS_PER_GROUP)
            col = direction * SSD_HEADS + g * SSD_HEADS_PER_GROUP + head
            pm[g * DIRS_X_HEADS + j, col] = 1.0
            lane0 = g * LANES
            pa[col, lane0 + j] = 1.0
            for part in range(2):
                pe[part, col, lane0 + part * 2 * DIRS_X_HEADS + j] = 1.0
                pw[part, col, lane0 + part * 2 * DIRS_X_HEADS + DIRS_X_HEADS + j] = 1.0
    HP = HEAD_GROUP_LANES
    ef = np.zeros((LANES, 2 * HP), np.float32)
    eb = np.zeros((LANES, 2 * HP), np.float32)
    for part in range(2):
        base = part * 2 * DIRS_X_HEADS
        for h in range(SSD_HEADS_PER_GROUP):
            cols = slice(h * SSD_HEAD_DIM, (h + 1) * SSD_HEAD_DIM)
            cols_w = slice(HP + h * SSD_HEAD_DIM, HP + (h + 1) * SSD_HEAD_DIM)
            ef[base + h, cols] = 1.0
            eb[base + SSD_HEADS_PER_GROUP + h, cols] = 1.0
            ef[base + DIRS_X_HEADS + h, cols_w] = 1.0
            eb[base + DIRS_X_HEADS + SSD_HEADS_PER_GROUP + h, cols_w] = 1.0
    as_bf16 = lambda a: jnp.asarray(a, BF16)
    return as_bf16(pa), as_bf16(pe), as_bf16(pw), as_bf16(pm), as_bf16(np.concatenate([ef, eb], axis=1))


def kernel(x, w_in, pool_w, pool_scale, ssd_conv_w, ssd_conv_b, a_log, dt_bias, d_skip, ssd_norm_g,
           w_ssd_proj, w_out, ln1_g, ln1_b, w_up, ffn_conv_w, ffn_conv_b, w_down, ln2_g, ln2_b):
    bsz, s, d = x.shape
    depth = w_in.shape[0]
    t = bsz * s
    alpha = float((2 * depth) ** 0.25)
    pool_width = pool_w.shape[1] * pool_w.shape[2]
    inner = w_ssd_proj.shape[1]
    conv_ch = ssd_conv_w.shape[2]
    n_dt = 2 * SSD_HEADS
    c0, c1, c2, c3 = pool_width, pool_width + inner, pool_width + inner + conv_ch, \
        pool_width + inner + conv_ch + n_dt
    tm = min(1024, t)
    pa, pe, pw, pm, e_all = _placement_constants()

    w_in_t = jnp.swapaxes(w_in, 1, 2)
    xf = x.reshape(t, d)
    xb3 = x.astype(BF16)
    for i in range(depth):
        w_gate_t = w_in_t[i, c3:][None]
        rows, acol, colfac = _dt_call(xb3, w_in_t, i, c2, dt_bias[i].reshape(1, -1),
                                      a_log[i].reshape(1, -1), pa, pe, pw, pm)
        pool_out = _pool_call(xb3, w_in_t, i, pool_w, pool_scale[i].reshape(1, -1))
        zs = _act_mm_call(xb3, w_in_t, i, c0, inner, True, 512, "z")
        gates = _act_mm_call(xb3, w_gate_t, 0, 0, 2 * d, False, 512, "gates").reshape(t, 2 * d)
        xbc = _xbc_call(xb3, w_in_t, i, c1, ssd_conv_w[i], ssd_conv_b[i].reshape(1, -1), 512)
        yn = _ssd_call(xbc, zs, rows, acol, colfac, e_all,
                       jnp.repeat(d_skip[i], SSD_HEAD_DIM).reshape(1, -1),
                       ssd_norm_g[i].reshape(1, -1))
        xf, xb = _merge_call(yn.reshape(t, inner), pool_out.reshape(t, d), gates, xf,
                             w_ssd_proj, w_out, i, ln1_g[i].reshape(1, -1), ln1_b[i].reshape(1, -1),
                             alpha, min(512, t))
        act = _ffn_up_call(xb.reshape(bsz, s, d), w_up, i, ffn_conv_w[i],
                           ffn_conv_b[i].reshape(1, -1), 256)
        xf, xb = _ffn_down_call(act.reshape(t, -1), xf, w_down, i,
                                ln2_g[i].reshape(1, -1), ln2_b[i].reshape(1, -1), alpha, tm)
        xb3 = xb.reshape(bsz, s, d)
    return xf.reshape(bsz, s, d)
```

```python
import functools

import numpy as np
import jax
import jax.numpy as jnp
from jax import lax
from jax.experimental import pallas as pl
from jax.experimental.pallas import tpu as pltpu

F32 = jnp.float32
BF16 = jnp.bfloat16

POOL_WINDOWS = (2, 4, 8, 16)
N_POOL_GROUPS = 4
SSD_HEAD_DIM = 64
SSD_GROUPS = 4
SSD_HEADS_PER_GROUP = 8
SSD_HEADS = SSD_GROUPS * SSD_HEADS_PER_GROUP
SSD_STATE = 128
SSD_CHUNK = 128
LN_EPS = 1e-5
RMS_EPS = 1e-5

LANES = 128
SUBLANES = 8
VMEM_LIMIT_BYTES = 56 * 1024 * 1024

CONV_PAD_ROWS = SUBLANES
CONV_PHASES = 4
POOL_BLOCK = 256
POOL_HALO = 16
HEAD_GROUP_LANES = SSD_HEADS_PER_GROUP * SSD_HEAD_DIM
DIRS_X_HEADS = 2 * SSD_HEADS_PER_GROUP
LOCAL_UNROLL = 16
SCAN_UNROLL = 8
ROW_SUB_BLOCK = 256

def _cparams(*sem):
    return pltpu.CompilerParams(dimension_semantics=sem, vmem_limit_bytes=VMEM_LIMIT_BYTES)


def _sigmoid(v):
    return 1.0 / (1.0 + jnp.exp(-v))


def _silu_tanh(v):
    h = 0.5 * v
    return h + h * jnp.tanh(h)


def _softplus(v):
    return jnp.maximum(v, 0.0) + jnp.log1p(jnp.exp(-jnp.abs(v)))


def _dot(a, b):
    return jnp.dot(a, b, preferred_element_type=F32)


def _dot_t(a, b_t):
    return lax.dot_general(a, b_t, (((1,), (1,)), ((), ())), preferred_element_type=F32)


def _split3(v):
    hi = v.astype(BF16)
    r1 = v - hi.astype(F32)
    mid = r1.astype(BF16)
    lo = (r1 - mid.astype(F32)).astype(BF16)
    return hi, mid, lo


def _dot_exact_rhs(m, v):
    hi, mid, lo = _split3(v)
    return _dot(m, hi) + _dot(m, mid) + _dot(m, lo)


def _dot_exact_lhs(v, m):
    hi, mid, lo = _split3(v)
    return _dot(hi, m) + _dot(mid, m) + _dot(lo, m)


def _layer_norm(r, g, b):
    mu = jnp.mean(r, axis=-1, keepdims=True)
    d = r - mu
    var = jnp.mean(d * d, axis=-1, keepdims=True)
    return d * lax.rsqrt(var + LN_EPS) * g + b


def _transpose_exact(v, perm):
    tdot = lambda p: lax.dot_general(perm, p, (((1,), (1,)), ((), ())), preferred_element_type=F32)
    hi, mid, lo = _split3(v)
    return tdot(hi) + tdot(mid) + tdot(lo)


def _dt_kernel(x_ref, wdt_ref, brow_ref, alrow_ref, pa_ref, pe_ref, pw_ref, pm_ref,
               rows_ref, acol_ref, colfac_ref, *, n_chunks):
    L = SSD_CHUNK
    hpg = SSD_HEADS_PER_GROUP
    dt_c = _softplus(_dot_t(x_ref[0], wdt_ref[0].astype(BF16)) + brow_ref[...])
    adt_c = dt_c * (-jnp.exp(alrow_ref[...]))
    ri = lax.broadcasted_iota(jnp.int32, (L, L), 0)
    ci = lax.broadcasted_iota(jnp.int32, (L, L), 1)
    lower = (ri >= ci).astype(BF16)
    upper = (ri <= ci).astype(BF16)
    nh = dt_c.shape[1]
    perm = pm_ref[...]
    isb_row = lax.broadcasted_iota(jnp.int32, (1, nh), 1) >= SSD_HEADS
    cums = []
    for c in range(n_chunks):
        a_c = adt_c[c * L:(c + 1) * L]
        cums.append(jnp.where(isb_row, _dot_exact_rhs(upper, a_c), _dot_exact_rhs(lower, a_c)))
    tots = [jnp.broadcast_to(jnp.where(isb_row, cm[0:1], cm[L - 1:L]), cm.shape) for cm in cums]
    cum = jnp.concatenate(cums, axis=0)
    tot = jnp.concatenate(tots, axis=0)
    e_a = jnp.exp(cum)
    w_in = dt_c * jnp.exp(tot - cum)
    e_hi = e_a.astype(BF16)
    e_lo = (e_a - e_hi.astype(F32)).astype(BF16)
    w_hi = w_in.astype(BF16)
    w_lo = (w_in - w_hi.astype(F32)).astype(BF16)
    acol_all = _dot_exact_lhs(cum, pa_ref[...])
    colfac_all = (_dot(e_hi, pe_ref[0]) + _dot(w_hi, pw_ref[0])
                  + _dot(e_lo, pe_ref[1]) + _dot(w_lo, pw_ref[1])).astype(BF16)
    for g in range(SSD_GROUPS):
        acol_ref[0, g] = acol_all[:, g * LANES:(g + 1) * LANES]
        colfac_ref[0, g] = colfac_all[:, g * LANES:(g + 1) * LANES]
    cum_r = _transpose_exact(cum, perm)
    dt_r = _transpose_exact(dt_c, perm)
    row0 = cum_r - jnp.log(dt_r)
    pieces = []
    for g in range(SSD_GROUPS):
        both = jnp.log(dt_r[2 * g * hpg:(2 * g + 1) * hpg] + dt_r[(2 * g + 1) * hpg:(2 * g + 2) * hpg])
        pieces += [both, both]
    row1 = jnp.concatenate(pieces, axis=0)
    for c in range(n_chunks):
        rows_ref[0, c, 0] = row0[:, c * L:(c + 1) * L]
        rows_ref[0, c, 1] = row1[:, c * L:(c + 1) * L]


def _dt_call(xb, w_in_t, layer, row0, brow, alrow, pa, pe, pw, pm):
    bsz, s, d = xb.shape
    nc = s // SSD_CHUNK
    nh = 2 * SSD_HEADS
    const = lambda *shape: pl.BlockSpec(shape, lambda b: (0,) * len(shape))
    return pl.pallas_call(
        functools.partial(_dt_kernel, n_chunks=nc),
        grid=(bsz,),
        in_specs=[pl.BlockSpec((1, s, d), lambda b: (b, 0, 0)),
                  pl.BlockSpec((1, nh, d), lambda b: (layer, row0 // nh, 0)),
                  const(1, nh), const(1, nh),
                  const(nh, SSD_GROUPS * LANES), const(2, nh, SSD_GROUPS * LANES),
                  const(2, nh, SSD_GROUPS * LANES), const(nh, nh)],
        out_specs=[pl.BlockSpec((1, nc, 2, nh, SSD_CHUNK), lambda b: (b, 0, 0, 0, 0)),
                   pl.BlockSpec((1, SSD_GROUPS, s, LANES), lambda b: (b, 0, 0, 0)),
                   pl.BlockSpec((1, SSD_GROUPS, s, LANES), lambda b: (b, 0, 0, 0))],
        out_shape=[jax.ShapeDtypeStruct((bsz, nc, 2, nh, SSD_CHUNK), F32),
                   jax.ShapeDtypeStruct((bsz, SSD_GROUPS, s, LANES), F32),
                   jax.ShapeDtypeStruct((bsz, SSD_GROUPS, s, LANES), BF16)],
        compiler_params=_cparams("parallel"),
        name="dt",
    )(xb, w_in_t, brow, alrow, pa, pe, pw, pm)


def _pool_kernel(x_ref, w_ref, wmap_ref, scale_ref, o_ref, u_ref, ub_ref, *, seq):
    g = pl.program_id(1)
    half = jnp.left_shift(1, g)
    u = _dot_t(x_ref[0], w_ref[0].astype(BF16))
    u_ref[...] = u
    ub_ref[...] = u.astype(BF16)
    blk = min(POOL_BLOCK, seq)
    win = min(seq, blk + 2 * POOL_HALO)
    for r in range(seq // blk):
        start = max(0, min(r * blk - POOL_HALO, seq - win))
        t = r * blk + lax.broadcasted_iota(jnp.int32, (blk, win), 0)
        j = start + lax.broadcasted_iota(jnp.int32, (blk, win), 1)
        band = ((j >= t - half) & (j <= t + half - 1)).astype(BF16)
        ssum = _dot(band, ub_ref[start:start + win, :])
        tq = r * blk + lax.broadcasted_iota(jnp.int32, ssum.shape, 0)
        cnt = jnp.minimum(tq + half - 1, seq - 1) - jnp.maximum(tq - half, 0) + 1
        pooled = ssum / cnt.astype(F32) - u_ref[r * blk:(r + 1) * blk, :]
        out = _dot(pooled.astype(BF16), wmap_ref[0, 0].astype(BF16)) * scale_ref[...]
        o_ref[0, r * blk:(r + 1) * blk, :] = out.astype(o_ref.dtype)


def _pool_call(xb, w_in_t, layer, wmap, scale):
    bsz, s, d = xb.shape
    pg = wmap.shape[-1]
    return pl.pallas_call(
        functools.partial(_pool_kernel, seq=s),
        grid=(bsz, N_POOL_GROUPS),
        in_specs=[pl.BlockSpec((1, s, d), lambda b, g: (b, 0, 0)),
                  pl.BlockSpec((1, pg, d), lambda b, g: (layer, g, 0)),
                  pl.BlockSpec((1, 1, pg, pg), lambda b, g: (layer, g, 0, 0)),
                  pl.BlockSpec((1, pg), lambda b, g: (0, g))],
        out_specs=pl.BlockSpec((1, s, pg), lambda b, g: (b, 0, g)),
        out_shape=jax.ShapeDtypeStruct((bsz, s, N_POOL_GROUPS * pg), BF16),
        scratch_shapes=[pltpu.VMEM((s, pg), F32), pltpu.VMEM((s, pg), BF16)],
        compiler_params=_cparams("parallel", "arbitrary"),
        name="pool",
    )(xb, w_in_t, wmap, scale)


def _act_mm_kernel(x_ref, w_ref, o_ref, *, silu, sub_rows):
    w = w_ref[0].astype(BF16)
    for r in range(x_ref.shape[1] // sub_rows):
        rows = slice(r * sub_rows, (r + 1) * sub_rows)
        h = _dot_t(x_ref[0, rows, :], w)
        sg = _sigmoid(h)
        o_ref[0, rows, :] = (sg * h if silu else sg).astype(o_ref.dtype)


def _act_mm_call(xb, w_t, layer, col0, n_cols, silu, tn, name):
    bsz, s, d = xb.shape
    return pl.pallas_call(
        functools.partial(_act_mm_kernel, silu=silu, sub_rows=min(ROW_SUB_BLOCK, s)),
        grid=(bsz, n_cols // tn),
        in_specs=[pl.BlockSpec((1, s, d), lambda i, j: (i, 0, 0)),
                  pl.BlockSpec((1, tn, d), lambda i, j: (layer, col0 // tn + j, 0))],
        out_specs=pl.BlockSpec((1, s, tn), lambda i, j: (i, 0, j)),
        out_shape=jax.ShapeDtypeStruct((bsz, s, n_cols), BF16),
        compiler_params=_cparams("parallel", "arbitrary"),
        name=name,
    )(xb, w_t)


def _fill_padded(hp_ref, x_ref, w, seq, mm_rows, blocks, transposed_w=False):
    if 0 in blocks:
        zeros = jnp.zeros((CONV_PAD_ROWS, LANES), F32)
        for sl in range(hp_ref.shape[0]):
            hp_ref[sl, 0:CONV_PAD_ROWS, :] = zeros
            hp_ref[sl, CONV_PAD_ROWS + seq:2 * CONV_PAD_ROWS + seq, :] = zeros
    for r in blocks:
        res = (_dot_t if transposed_w else _dot)(x_ref[0, r * mm_rows:(r + 1) * mm_rows, :], w)
        for sl in range(hp_ref.shape[0]):
            hp_ref[sl, CONV_PAD_ROWS + r * mm_rows:CONV_PAD_ROWS + (r + 1) * mm_rows, :] = (
                res[:, sl * LANES:(sl + 1) * LANES])


def _conv_phases(hp_ref, sl, cw_ref, b_ref, row0, rows, taps, lanes):
    pad = taps // 2
    n = rows // CONV_PHASES
    shifted = {m: hp_ref[sl, pl.ds(CONV_PAD_ROWS + row0 + m, n, stride=CONV_PHASES), :]
               for m in range(-pad, CONV_PHASES + pad)}
    outs = []
    for j in range(CONV_PHASES):
        acc = b_ref[:, lanes]
        for k in range(taps):
            acc = acc + cw_ref[k:k + 1, lanes] * shifted[j + k - pad]
        outs.append(acc)
    return outs


def _store_phases(st_ref, sl, row0, rows, phases):
    for j, v in enumerate(phases):
        st_ref[sl, pl.ds(row0 + j, rows // CONV_PHASES, stride=CONV_PHASES), :] = v


def _xbc_kernel(x_ref, w_ref, cw_ref, b_ref, o_ref, *scratch, seq, taps, sub_cols):
    n_sub = len(scratch) // 2
    slabs = sub_cols // LANES
    mm_rows = min(512, seq)
    rows = min(256, seq)
    n_mm = seq // mm_rows
    ws = [w_ref[0, q * sub_cols:(q + 1) * sub_cols, :].astype(BF16) for q in range(n_sub)]

    def conv_block(q, r):
        hp_ref, st_ref = scratch[2 * q], scratch[2 * q + 1]
        for sl in range(slabs):
            gsl = q * slabs + sl
            lanes = slice(gsl * LANES, (gsl + 1) * LANES)
            phases = _conv_phases(hp_ref, sl, cw_ref, b_ref, r * rows, rows, taps, lanes)
            _store_phases(st_ref, sl, r * rows, rows, [_silu_tanh(v) for v in phases])
            o_ref[0, r * rows:(r + 1) * rows, lanes] = (
                st_ref[sl, r * rows:(r + 1) * rows, :].astype(o_ref.dtype))

    _fill_padded(scratch[0], x_ref, ws[0], seq, mm_rows, range(n_mm), transposed_w=True)
    for q in range(n_sub):
        for m in range(n_mm):
            if q + 1 < n_sub:
                _fill_padded(scratch[2 * q + 2], x_ref, ws[q + 1], seq, mm_rows, [m], transposed_w=True)
            for r in range(m * mm_rows // rows, (m + 1) * mm_rows // rows):
                conv_block(q, r)


def _xbc_call(xb, w_in_t, layer, col0, conv_w, conv_b, tn):
    bsz, s, d = xb.shape
    n = conv_w.shape[1]
    taps = conv_w.shape[0]
    sub_cols = 4 * LANES
    sub_scratch = [pltpu.VMEM((sub_cols // LANES, s + 2 * CONV_PAD_ROWS, LANES), F32),
                   pltpu.VMEM((sub_cols // LANES, s, LANES), F32)]
    return pl.pallas_call(
        functools.partial(_xbc_kernel, seq=s, taps=taps, sub_cols=sub_cols),
        grid=(bsz, n // tn),
        in_specs=[pl.BlockSpec((1, s, d), lambda b, j: (b, 0, 0)),
                  pl.BlockSpec((1, tn, d), lambda b, j: (layer, col0 // tn + j, 0)),
                  pl.BlockSpec((taps, tn), lambda b, j: (0, j)),
                  pl.BlockSpec((1, tn), lambda b, j: (0, j))],
        out_specs=pl.BlockSpec((1, s, tn), lambda b, j: (b, 0, j)),
        out_shape=jax.ShapeDtypeStruct((bsz, s, n), BF16),
        scratch_shapes=sub_scratch * (tn // sub_cols),
        compiler_params=_cparams("parallel", "arbitrary"),
        name="xbc",
    )(xb, w_in_t, conv_w, conv_b)


def _ssd_kernel(x_ref, b_ref, c_ref, zs_ref, rows_ref, acol_ref, colfac_ref, e_ref,
                dskip_ref, gain_ref, o_ref, y_ref, xwf_ref, xwb_ref, scf_ref, scb_ref,
                etf_ref, etb_ref, sf_ref, sb_ref, *, n_chunks):
    L = SSD_CHUNK
    HP = HEAD_GROUP_LANES
    HPG = SSD_HEADS_PER_GROUP
    li = lax.broadcasted_iota(jnp.int32, (L, L), 0)
    si = lax.broadcasted_iota(jnp.int32, (L, L), 1)
    lower = si < li
    upper = si > li
    lane = lax.broadcasted_iota(jnp.int32, (L, 2 * SSD_HEAD_DIM), 1)
    first_head = lane < SSD_HEAD_DIM

    def chunk_local(c):
        rc = pl.ds(pl.multiple_of(c * L, L), L)
        xc = x_ref[0, rc, :]
        bc = b_ref[0, rc, :]
        cc = c_ref[0, rc, :]
        scores = lax.dot_general(cc, bc, (((1,), (1,)), ((), ())), preferred_element_type=F32)
        arow = rows_ref[0, c, 0]
        drow = rows_ref[0, c, 1]
        acol = acol_ref[0, 0, rc, :]
        ypairs = []
        for p in range(HPG // 2):
            dmats = []
            for h in (2 * p, 2 * p + 1):
                hb = HPG + h
                seg = jnp.where(lower, acol[:, h:h + 1] - arow[h:h + 1, :],
                                jnp.where(upper, acol[:, hb:hb + 1] - arow[hb:hb + 1, :],
                                          drow[h:h + 1, :]))
                dmats.append((scores * jnp.exp(seg)).astype(BF16))
            dpair = jnp.concatenate(dmats, axis=1)
            xpair = xc[:, p * 2 * SSD_HEAD_DIM:(p + 1) * 2 * SSD_HEAD_DIM]
            zero = jnp.zeros_like(xpair)
            xbd = jnp.concatenate([jnp.where(first_head, xpair, zero),
                                   jnp.where(first_head, zero, xpair)], axis=0)
            ypairs.append(_dot(dpair, xbd))
        xf = xc.astype(F32)
        y_ref[rc, :] = jnp.concatenate(ypairs, axis=1) + dskip_ref[...] * xf

        fac = _dot(colfac_ref[0, 0, rc, :], e_ref[...])
        scf_ref[rc, :] = fac[:, 0 * HP:1 * HP]
        xwf_ref[rc, :] = (xf * fac[:, 1 * HP:2 * HP]).astype(BF16)
        scb_ref[rc, :] = fac[:, 2 * HP:3 * HP]
        xwb_ref[rc, :] = (xf * fac[:, 3 * HP:4 * HP]).astype(BF16)
        etf_ref[c] = jnp.broadcast_to(fac[L - 1:L, 0 * HP:1 * HP], (SUBLANES, HP))
        etb_ref[c] = jnp.broadcast_to(fac[0:1, 2 * HP:3 * HP], (SUBLANES, HP))

    local_unroll = LOCAL_UNROLL if n_chunks % LOCAL_UNROLL == 0 else 1

    def local_step(i, carry):
        for u in range(local_unroll):
            chunk_local(i * local_unroll + u)
        return carry

    lax.fori_loop(0, n_chunks // local_unroll, local_step, 0)

    sf_ref[...] = jnp.zeros(sf_ref.shape, F32)
    sb_ref[...] = jnp.zeros(sb_ref.shape, F32)

    def scan_one(c, s_ref, xw_ref, sc_ref, et_ref):
        rc = pl.ds(pl.multiple_of(c * L, L), L)
        s_old = s_ref[...]
        y_ref[rc, :] += _dot(c_ref[0, rc, :], s_old.astype(BF16)) * sc_ref[rc, :]
        upd = lax.dot_general(b_ref[0, rc, :], xw_ref[rc, :], (((0,), (0,)), ((), ())),
                              preferred_element_type=F32)
        s_ref[...] = s_old * et_ref[c][0:1, :] + upd

    def scan_step(i, carry):
        scan_one(i, sf_ref, xwf_ref, scf_ref, etf_ref)
        scan_one(n_chunks - 1 - i, sb_ref, xwb_ref, scb_ref, etb_ref)
        return carry

    lax.fori_loop(0, n_chunks, scan_step, 0, unroll=SCAN_UNROLL if n_chunks % SCAN_UNROLL == 0 else 1)

    rows = 256 if (n_chunks * L) % 256 == 0 else L
    for r in range(n_chunks * L // rows):
        sl = slice(r * rows, (r + 1) * rows)
        v = y_ref[sl, :] * zs_ref[0, sl, :].astype(F32)
        v = v * lax.rsqrt(jnp.mean(v * v, axis=-1, keepdims=True) + RMS_EPS)
        o_ref[0, sl, :] = (v * gain_ref[...]).astype(o_ref.dtype)


def _ssd_call(xbc, zs, rows, acol, colfac, e_all, dskip, gain):
    bsz, s, _ = xbc.shape
    nc = s // SSD_CHUNK
    HP = HEAD_GROUP_LANES
    n_x_blocks = SSD_HEADS * SSD_HEAD_DIM // SSD_STATE
    return pl.pallas_call(
        functools.partial(_ssd_kernel, n_chunks=nc),
        grid=(bsz, SSD_GROUPS),
        in_specs=[pl.BlockSpec((1, s, HP), lambda b, g: (b, 0, g)),
                  pl.BlockSpec((1, s, SSD_STATE), lambda b, g: (b, 0, n_x_blocks + g)),
                  pl.BlockSpec((1, s, SSD_STATE), lambda b, g: (b, 0, n_x_blocks + SSD_GROUPS + g)),
                  pl.BlockSpec((1, s, HP), lambda b, g: (b, 0, g)),
                  pl.BlockSpec((1, nc, 2, DIRS_X_HEADS, SSD_CHUNK), lambda b, g: (b, 0, 0, g, 0)),
                  pl.BlockSpec((1, 1, s, LANES), lambda b, g: (b, g, 0, 0)),
                  pl.BlockSpec((1, 1, s, LANES), lambda b, g: (b, g, 0, 0)),
                  pl.BlockSpec((LANES, 4 * HP), lambda b, g: (0, 0)),
                  pl.BlockSpec((1, HP), lambda b, g: (0, g)),
                  pl.BlockSpec((1, HP), lambda b, g: (0, g))],
        out_specs=pl.BlockSpec((1, s, HP), lambda b, g: (b, 0, g)),
        out_shape=jax.ShapeDtypeStruct((bsz, s, SSD_GROUPS * HP), BF16),
        scratch_shapes=[pltpu.VMEM((s, HP), F32),
                        pltpu.VMEM((s, HP), BF16), pltpu.VMEM((s, HP), BF16),
                        pltpu.VMEM((s, HP), F32), pltpu.VMEM((s, HP), F32),
                        pltpu.VMEM((nc, SUBLANES, HP), F32), pltpu.VMEM((nc, SUBLANES, HP), F32),
                        pltpu.VMEM((SSD_STATE, HP), F32), pltpu.VMEM((SSD_STATE, HP), F32)],
        compiler_params=_cparams("parallel", "arbitrary"),
        name="ssd",
    )(xbc, xbc, xbc, zs, rows, acol, colfac, e_all, dskip, gain)


def _merge_kernel(yn_ref, pool_ref, g0_ref, g1_ref, x_ref, wsp32_ref, wout32_ref, lng_ref, lnb_ref,
                  of_ref, ob_ref, wsp_ref, wout_ref, *, alpha, sub_rows):
    @pl.when(pl.program_id(0) == 0)
    def _():
        wsp_ref[...] = wsp32_ref[0].astype(BF16)
        wout_ref[...] = wout32_ref[0].astype(BF16)

    for r in range(x_ref.shape[0] // sub_rows):
        rows = slice(r * sub_rows, (r + 1) * sub_rows)
        ssd_out = _dot(yn_ref[rows, :], wsp_ref[...])
        merged = (g0_ref[rows, :].astype(F32) * pool_ref[rows, :].astype(F32)
                  + g1_ref[rows, :].astype(F32) * ssd_out)
        mix = _dot(merged.astype(BF16), wout_ref[...])
        out = _layer_norm(alpha * x_ref[rows, :] + mix, lng_ref[...], lnb_ref[...])
        of_ref[rows, :] = out
        ob_ref[rows, :] = out.astype(BF16)


def _merge_call(yn, pool_out, gates, x, wsp, wout, layer, lng, lnb, alpha, tm):
    t, d = x.shape
    inner = yn.shape[1]
    row = lambda i: (i, 0)
    const = lambda i: (0, 0)
    resident = lambda shape: pl.BlockSpec(shape, lambda i: (layer, 0, 0), pipeline_mode=pl.Buffered(1))
    return pl.pallas_call(
        functools.partial(_merge_kernel, alpha=alpha, sub_rows=min(ROW_SUB_BLOCK, tm)),
        grid=(t // tm,),
        in_specs=[pl.BlockSpec((tm, inner), row),
                  pl.BlockSpec((tm, d), row),
                  pl.BlockSpec((tm, d), lambda i: (i, 0)),
                  pl.BlockSpec((tm, d), lambda i: (i, 1)),
                  pl.BlockSpec((tm, d), row),
                  resident((1, inner, d)),
                  resident((1, d, d)),
                  pl.BlockSpec((1, d), const),
                  pl.BlockSpec((1, d), const)],
        out_specs=[pl.BlockSpec((tm, d), row), pl.BlockSpec((tm, d), row)],
        out_shape=[jax.ShapeDtypeStruct((t, d), F32), jax.ShapeDtypeStruct((t, d), BF16)],
        scratch_shapes=[pltpu.VMEM((inner, d), BF16), pltpu.VMEM((d, d), BF16)],
        compiler_params=_cparams("arbitrary"),
        name="merge",
    )(yn, pool_out, gates, gates, x, wsp, wout, lng, lnb)


def _ffn_up_kernel(x_ref, wg_ref, wv_ref, cwg_ref, cwv_ref, bg_ref, bv_ref, o_ref, hg_ref, hv_ref,
                   st_ref, *, seq, taps):
    mm_rows = min(512, seq)
    _fill_padded(hg_ref, x_ref, wg_ref[0].astype(BF16), seq, mm_rows, range(seq // mm_rows))
    _fill_padded(hv_ref, x_ref, wv_ref[0].astype(BF16), seq, mm_rows, range(seq // mm_rows))
    rows = min(256, seq)
    half_cwv = 0.5 * cwv_ref[...]
    half_bv = 0.5 * bv_ref[...]
    for r in range(seq // rows):
        for sl in range(hg_ref.shape[0]):
            lanes = slice(sl * LANES, (sl + 1) * LANES)
            gates = _conv_phases(hg_ref, sl, cwg_ref, bg_ref, r * rows, rows, taps, lanes)
            half_vals = _conv_phases(hv_ref, sl, half_cwv, half_bv, r * rows, rows, taps, lanes)
            acts = [g * (1.0 + lax.erf(g * np.float32(2.0 ** -0.5))) * hv
                    for g, hv in zip(gates, half_vals)]
            _store_phases(st_ref, sl, r * rows, rows, acts)
            o_ref[0, r * rows:(r + 1) * rows, sl * LANES:(sl + 1) * LANES] = (
                st_ref[sl, r * rows:(r + 1) * rows, :].astype(o_ref.dtype))


def _ffn_up_call(xb, w_up, layer, conv_w, conv_b, tn):
    bsz, s, d = xb.shape
    dff = w_up.shape[2] // 2
    nt = dff // tn
    taps = conv_w.shape[0]
    return pl.pallas_call(
        functools.partial(_ffn_up_kernel, seq=s, taps=taps),
        grid=(bsz, nt),
        in_specs=[pl.BlockSpec((1, s, d), lambda b, j: (b, 0, 0)),
                  pl.BlockSpec((1, d, tn), lambda b, j: (layer, 0, j)),
                  pl.BlockSpec((1, d, tn), lambda b, j: (layer, 0, nt + j)),
                  pl.BlockSpec((taps, tn), lambda b, j: (0, j)),
                  pl.BlockSpec((taps, tn), lambda b, j: (0, nt + j)),
                  pl.BlockSpec((1, tn), lambda b, j: (0, j)),
                  pl.BlockSpec((1, tn), lambda b, j: (0, nt + j))],
        out_specs=pl.BlockSpec((1, s, tn), lambda b, j: (b, 0, j)),
        out_shape=jax.ShapeDtypeStruct((bsz, s, dff), BF16),
        scratch_shapes=[pltpu.VMEM((tn // LANES, s + 2 * CONV_PAD_ROWS, LANES), F32),
                        pltpu.VMEM((tn // LANES, s + 2 * CONV_PAD_ROWS, LANES), F32),
                        pltpu.VMEM((tn // LANES, s, LANES), F32)],
        compiler_params=_cparams("parallel", "arbitrary"),
        name="ffn_up",
    )(xb, w_up, w_up, conv_w, conv_w, conv_b, conv_b)


def _ffn_down_kernel(a_ref, x_ref, w32_ref, lng_ref, lnb_ref, of_ref, ob_ref, w_ref, *, alpha, sub_rows):
    @pl.when(pl.program_id(0) == 0)
    def _():
        w_ref[...] = w32_ref[0].astype(BF16)

    for r in range(x_ref.shape[0] // sub_rows):
        rows = slice(r * sub_rows, (r + 1) * sub_rows)
        out = _layer_norm(alpha * x_ref[rows, :] + _dot(a_ref[rows, :], w_ref[...]),
                          lng_ref[...], lnb_ref[...])
        of_ref[rows, :] = out
        ob_ref[rows, :] = out.astype(BF16)


def _ffn_down_call(act, x, w_down, layer, lng, lnb, alpha, tm):
    t, d = x.shape
    dff = act.shape[1]
    row = lambda i: (i, 0)
    const = lambda i: (0, 0)
    return pl.pallas_call(
        functools.partial(_ffn_down_kernel, alpha=alpha, sub_rows=min(ROW_SUB_BLOCK, tm)),
        grid=(t // tm,),
        in_specs=[pl.BlockSpec((tm, dff), row), pl.BlockSpec((tm, d), row),
                  pl.BlockSpec((1, dff, d), lambda i: (layer, 0, 0), pipeline_mode=pl.Buffered(1)),
                  pl.BlockSpec((1, d), const), pl.BlockSpec((1, d), const)],
        out_specs=[pl.BlockSpec((tm, d), row), pl.BlockSpec((tm, d), row)],
        out_shape=[jax.ShapeDtypeStruct((t, d), F32), jax.ShapeDtypeStruct((t, d), BF16)],
        scratch_shapes=[pltpu.VMEM((dff, d), BF16)],
        compiler_params=_cparams("arbitrary"),
        name="ffn_down",
    )(act, x, w_down, lng, lnb)


def _placement_constants():
    nh = 2 * SSD_HEADS
    pa = np.zeros((nh, SSD_GROUPS * LANES), np.float32)
    pe = np.zeros((2, nh, SSD_GROUPS * LANES), np.float32)
    pw = np.zeros((2, nh, SSD_GROUPS * LANES), np.float32)
    pm = np.zeros((nh, nh), np.float32)
    for g in range(SSD_GROUPS):
        for j in range(DIRS_X_HEADS):
            direction, head = divmod(j, SSD_HEADS_PER_GROUP)
            col = direction * SSD_HEADS + g * SSD_HEADS_PER_GROUP + head
            pm[g * DIRS_X_HEADS + j, col] = 1.0
            lane0 = g * LANES
            pa[col, lane0 + j] = 1.0
            for part in range(2):
                pe[part, col, lane0 + part * 2 * DIRS_X_HEADS + j] = 1.0
                pw[part, col, lane0 + part * 2 * DIRS_X_HEADS + DIRS_X_HEADS + j] = 1.0
    HP = HEAD_GROUP_LANES
    ef = np.zeros((LANES, 2 * HP), np.float32)
    eb = np.zeros((LANES, 2 * HP), np.float32)
    for part in range(2):
        base = part * 2 * DIRS_X_HEADS
        for h in range(SSD_HEADS_PER_GROUP):
            cols = slice(h * SSD_HEAD_DIM, (h + 1) * SSD_HEAD_DIM)
            cols_w = slice(HP + h * SSD_HEAD_DIM, HP + (h + 1) * SSD_HEAD_DIM)
            ef[base + h, cols] = 1.0
            eb[base + SSD_HEADS_PER_GROUP + h, cols] = 1.0
            ef[base + DIRS_X_HEADS + h, cols_w] = 1.0
            eb[base + DIRS_X_HEADS + SSD_HEADS_PER_GROUP + h, cols_w] = 1.0
    as_bf16 = lambda a: jnp.asarray(a, BF16)
    return as_bf16(pa), as_bf16(pe), as_bf16(pw), as_bf16(pm), as_bf16(np.concatenate([ef, eb], axis=1))


def kernel(x, w_in, pool_w, pool_scale, ssd_conv_w, ssd_conv_b, a_log, dt_bias, d_skip, ssd_norm_g,
           w_ssd_proj, w_out, ln1_g, ln1_b, w_up, ffn_conv_w, ffn_conv_b, w_down, ln2_g, ln2_b):
    bsz, s, d = x.shape
    depth = w_in.shape[0]
    t = bsz * s
    alpha = float((2 * depth) ** 0.25)
    pool_width = pool_w.shape[1] * pool_w.shape[2]
    inner = w_ssd_proj.shape[1]
    conv_ch = ssd_conv_w.shape[2]
    n_dt = 2 * SSD_HEADS
    c0, c1, c2, c3 = pool_width, pool_width + inner, pool_width + inner + conv_ch, \
        pool_width + inner + conv_ch + n_dt
    tm = min(1024, t)
    pa, pe, pw, pm, e_all = _placement_constants()

    w_in_t = jnp.swapaxes(w_in, 1, 2)
    xf = x.reshape(t, d)
    xb3 = x.astype(BF16)
    for i in range(depth):
        w_gate_t = w_in_t[i, c3:][None]
        rows, acol, colfac = _dt_call(xb3, w_in_t, i, c2, dt_bias[i].reshape(1, -1),
                                      a_log[i].reshape(1, -1), pa, pe, pw, pm)
        pool_out = _pool_call(xb3, w_in_t, i, pool_w, pool_scale[i].reshape(1, -1))
        zs = _act_mm_call(xb3, w_in_t, i, c0, inner, True, 1024, "z")
        gates = _act_mm_call(xb3, w_gate_t, 0, 0, 2 * d, False, 1024, "gates").reshape(t, 2 * d)
        xbc = _xbc_call(xb3, w_in_t, i, c1, ssd_conv_w[i], ssd_conv_b[i].reshape(1, -1), 512)
        yn = _ssd_call(xbc, zs, rows, acol, colfac, e_all,
                       jnp.repeat(d_skip[i], SSD_HEAD_DIM).reshape(1, -1),
                       ssd_norm_g[i].reshape(1, -1))
        xf, xb = _merge_call(yn.reshape(t, inner), pool_out.reshape(t, d), gates, xf,
                             w_ssd_proj, w_out, i, ln1_g[i].reshape(1, -1), ln1_b[i].reshape(1, -1),
                             alpha, min(512, t))
        act = _ffn_up_call(xb.reshape(bsz, s, d), w_up, i, ffn_conv_w[i],
                           ffn_conv_b[i].reshape(1, -1), 256)
        xf, xb = _ffn_down_call(act.reshape(t, -1), xf, w_down, i,
                                ln2_g[i].reshape(1, -1), ln2_b[i].reshape(1, -1), alpha, tm)
        xb3 = xb.reshape(bsz, s, d)
    return xf.reshape(bsz, s, d)
```

```python
import functools

import numpy as np
import jax
import jax.numpy as jnp
from jax import lax
from jax.experimental import pallas as pl
from jax.experimental.pallas import tpu as pltpu

F32 = jnp.float32
BF16 = jnp.bfloat16

POOL_WINDOWS = (2, 4, 8, 16)
N_POOL_GROUPS = 4
SSD_HEAD_DIM = 64
SSD_GROUPS = 4
SSD_HEADS_PER_GROUP = 8
SSD_HEADS = SSD_GROUPS * SSD_HEADS_PER_GROUP
SSD_STATE = 128
SSD_CHUNK = 128
LN_EPS = 1e-5
RMS_EPS = 1e-5

LANES = 128
SUBLANES = 8
VMEM_LIMIT_BYTES = 56 * 1024 * 1024

CONV_PAD_ROWS = SUBLANES
CONV_PHASES = 4
POOL_BLOCK = 256
POOL_HALO = 16
HEAD_GROUP_LANES = SSD_HEADS_PER_GROUP * SSD_HEAD_DIM
DIRS_X_HEADS = 2 * SSD_HEADS_PER_GROUP
LOCAL_UNROLL = 16
SCAN_UNROLL = 8
ROW_SUB_BLOCK = 256

def _cparams(*sem):
    return pltpu.CompilerParams(dimension_semantics=sem, vmem_limit_bytes=VMEM_LIMIT_BYTES)


def _sigmoid(v):
    return 1.0 / (1.0 + jnp.exp(-v))


def _silu_tanh(v):
    h = 0.5 * v
    return h + h * jnp.tanh(h)


def _softplus(v):
    return jnp.maximum(v, 0.0) + jnp.log1p(jnp.exp(-jnp.abs(v)))


def _dot(a, b):
    return jnp.dot(a, b, preferred_element_type=F32)


def _dot_t(a, b_t):
    return lax.dot_general(a, b_t, (((1,), (1,)), ((), ())), preferred_element_type=F32)


def _split3(v):
    hi = v.astype(BF16)
    r1 = v - hi.astype(F32)
    mid = r1.astype(BF16)
    lo = (r1 - mid.astype(F32)).astype(BF16)
    return hi, mid, lo


def _dot_exact_rhs(m, v):
    hi, mid, lo = _split3(v)
    return _dot(m, hi) + _dot(m, mid) + _dot(m, lo)


def _dot_exact_lhs(v, m):
    hi, mid, lo = _split3(v)
    return _dot(hi, m) + _dot(mid, m) + _dot(lo, m)


def _layer_norm(r, g, b):
    mu = jnp.mean(r, axis=-1, keepdims=True)
    d = r - mu
    var = jnp.mean(d * d, axis=-1, keepdims=True)
    return d * lax.rsqrt(var + LN_EPS) * g + b


def _transpose_exact(v, perm):
    tdot = lambda p: lax.dot_general(perm, p, (((1,), (1,)), ((), ())), preferred_element_type=F32)
    hi, mid, lo = _split3(v)
    return tdot(hi) + tdot(mid) + tdot(lo)


def _dt_kernel(x_ref, wdt_ref, brow_ref, alrow_ref, pa_ref, pe_ref, pw_ref, pm_ref,
               rows_ref, acol_ref, colfac_ref, *, n_chunks):
    L = SSD_CHUNK
    hpg = SSD_HEADS_PER_GROUP
    dt_c = _softplus(_dot_t(x_ref[0], wdt_ref[0].astype(BF16)) + brow_ref[...])
    adt_c = dt_c * (-jnp.exp(alrow_ref[...]))
    ri = lax.broadcasted_iota(jnp.int32, (L, L), 0)
    ci = lax.broadcasted_iota(jnp.int32, (L, L), 1)
    lower = (ri >= ci).astype(BF16)
    upper = (ri <= ci).astype(BF16)
    nh = dt_c.shape[1]
    perm = pm_ref[...]
    isb_row = lax.broadcasted_iota(jnp.int32, (1, nh), 1) >= SSD_HEADS
    cums = []
    for c in range(n_chunks):
        a_c = adt_c[c * L:(c + 1) * L]
        cums.append(jnp.where(isb_row, _dot_exact_rhs(upper, a_c), _dot_exact_rhs(lower, a_c)))
    tots = [jnp.broadcast_to(jnp.where(isb_row, cm[0:1], cm[L - 1:L]), cm.shape) for cm in cums]
    cum = jnp.concatenate(cums, axis=0)
    tot = jnp.concatenate(tots, axis=0)
    e_a = jnp.exp(cum)
    w_in = dt_c * jnp.exp(tot - cum)
    e_hi = e_a.astype(BF16)
    e_lo = (e_a - e_hi.astype(F32)).astype(BF16)
    w_hi = w_in.astype(BF16)
    w_lo = (w_in - w_hi.astype(F32)).astype(BF16)
    acol_all = _dot_exact_lhs(cum, pa_ref[...])
    colfac_all = (_dot(e_hi, pe_ref[0]) + _dot(w_hi, pw_ref[0])
                  + _dot(e_lo, pe_ref[1]) + _dot(w_lo, pw_ref[1])).astype(BF16)
    for g in range(SSD_GROUPS):
        acol_ref[0, g] = acol_all[:, g * LANES:(g + 1) * LANES]
        colfac_ref[0, g] = colfac_all[:, g * LANES:(g + 1) * LANES]
    cum_r = _transpose_exact(cum, perm)
    dt_r = _transpose_exact(dt_c, perm)
    row0 = cum_r - jnp.log(dt_r)
    pieces = []
    for g in range(SSD_GROUPS):
        both = jnp.log(dt_r[2 * g * hpg:(2 * g + 1) * hpg] + dt_r[(2 * g + 1) * hpg:(2 * g + 2) * hpg])
        pieces += [both, both]
    row1 = jnp.concatenate(pieces, axis=0)
    for c in range(n_chunks):
        rows_ref[0, c, 0] = row0[:, c * L:(c + 1) * L]
        rows_ref[0, c, 1] = row1[:, c * L:(c + 1) * L]


def _dt_call(xb, w_in_t, layer, row0, brow, alrow, pa, pe, pw, pm):
    bsz, s, d = xb.shape
    nc = s // SSD_CHUNK
    nh = 2 * SSD_HEADS
    const = lambda *shape: pl.BlockSpec(shape, lambda b: (0,) * len(shape))
    return pl.pallas_call(
        functools.partial(_dt_kernel, n_chunks=nc),
        grid=(bsz,),
        in_specs=[pl.BlockSpec((1, s, d), lambda b: (b, 0, 0)),
                  pl.BlockSpec((1, nh, d), lambda b: (layer, row0 // nh, 0)),
                  const(1, nh), const(1, nh),
                  const(nh, SSD_GROUPS * LANES), const(2, nh, SSD_GROUPS * LANES),
                  const(2, nh, SSD_GROUPS * LANES), const(nh, nh)],
        out_specs=[pl.BlockSpec((1, nc, 2, nh, SSD_CHUNK), lambda b: (b, 0, 0, 0, 0)),
                   pl.BlockSpec((1, SSD_GROUPS, s, LANES), lambda b: (b, 0, 0, 0)),
                   pl.BlockSpec((1, SSD_GROUPS, s, LANES), lambda b: (b, 0, 0, 0))],
        out_shape=[jax.ShapeDtypeStruct((bsz, nc, 2, nh, SSD_CHUNK), F32),
                   jax.ShapeDtypeStruct((bsz, SSD_GROUPS, s, LANES), F32),
                   jax.ShapeDtypeStruct((bsz, SSD_GROUPS, s, LANES), BF16)],
        compiler_params=_cparams("parallel"),
        name="dt",
    )(xb, w_in_t, brow, alrow, pa, pe, pw, pm)


def _pool_kernel(x_ref, w_ref, wmap_ref, scale_ref, o_ref, u_ref, ub_ref, *, seq):
    g = pl.program_id(1)
    half = jnp.left_shift(1, g)
    u = _dot_t(x_ref[0], w_ref[0].astype(BF16))
    u_ref[...] = u
    ub_ref[...] = u.astype(BF16)
    blk = min(POOL_BLOCK, seq)
    win = min(seq, blk + 2 * POOL_HALO)
    for r in range(seq // blk):
        start = max(0, min(r * blk - POOL_HALO, seq - win))
        t = r * blk + lax.broadcasted_iota(jnp.int32, (blk, win), 0)
        j = start + lax.broadcasted_iota(jnp.int32, (blk, win), 1)
        band = ((j >= t - half) & (j <= t + half - 1)).astype(BF16)
        ssum = _dot(band, ub_ref[start:start + win, :])
        tq = r * blk + lax.broadcasted_iota(jnp.int32, ssum.shape, 0)
        cnt = jnp.minimum(tq + half - 1, seq - 1) - jnp.maximum(tq - half, 0) + 1
        pooled = ssum / cnt.astype(F32) - u_ref[r * blk:(r + 1) * blk, :]
        out = _dot(pooled.astype(BF16), wmap_ref[0, 0].astype(BF16)) * scale_ref[...]
        o_ref[0, r * blk:(r + 1) * blk, :] = out.astype(o_ref.dtype)


def _pool_call(xb, w_in_t, layer, wmap, scale):
    bsz, s, d = xb.shape
    pg = wmap.shape[-1]
    return pl.pallas_call(
        functools.partial(_pool_kernel, seq=s),
        grid=(bsz, N_POOL_GROUPS),
        in_specs=[pl.BlockSpec((1, s, d), lambda b, g: (b, 0, 0)),
                  pl.BlockSpec((1, pg, d), lambda b, g: (layer, g, 0)),
                  pl.BlockSpec((1, 1, pg, pg), lambda b, g: (layer, g, 0, 0)),
                  pl.BlockSpec((1, pg), lambda b, g: (0, g))],
        out_specs=pl.BlockSpec((1, s, pg), lambda b, g: (b, 0, g)),
        out_shape=jax.ShapeDtypeStruct((bsz, s, N_POOL_GROUPS * pg), BF16),
        scratch_shapes=[pltpu.VMEM((s, pg), F32), pltpu.VMEM((s, pg), BF16)],
        compiler_params=_cparams("parallel", "arbitrary"),
        name="pool",
    )(xb, w_in_t, wmap, scale)


def _act_mm_kernel(x_ref, w_ref, o_ref, *, silu, sub_rows):
    w = w_ref[0].astype(BF16)
    for r in range(x_ref.shape[1] // sub_rows):
        rows = slice(r * sub_rows, (r + 1) * sub_rows)
        h = _dot_t(x_ref[0, rows, :], w)
        sg = _sigmoid(h)
        o_ref[0, rows, :] = (sg * h if silu else sg).astype(o_ref.dtype)


def _act_mm_call(xb, w_t, layer, col0, n_cols, silu, tn, name):
    bsz, s, d = xb.shape
    return pl.pallas_call(
        functools.partial(_act_mm_kernel, silu=silu, sub_rows=min(ROW_SUB_BLOCK, s)),
        grid=(bsz, n_cols // tn),
        in_specs=[pl.BlockSpec((1, s, d), lambda i, j: (i, 0, 0)),
                  pl.BlockSpec((1, tn, d), lambda i, j: (layer, col0 // tn + j, 0))],
        out_specs=pl.BlockSpec((1, s, tn), lambda i, j: (i, 0, j)),
        out_shape=jax.ShapeDtypeStruct((bsz, s, n_cols), BF16),
        compiler_params=_cparams("parallel", "arbitrary"),
        name=name,
    )(xb, w_t)


def _fill_padded(hp_ref, x_ref, w, seq, mm_rows, blocks, transposed_w=False):
    if 0 in blocks:
        zeros = jnp.zeros((CONV_PAD_ROWS, LANES), F32)
        for sl in range(hp_ref.shape[0]):
            hp_ref[sl, 0:CONV_PAD_ROWS, :] = zeros
            hp_ref[sl, CONV_PAD_ROWS + seq:2 * CONV_PAD_ROWS + seq, :] = zeros
    for r in blocks:
        res = (_dot_t if transposed_w else _dot)(x_ref[0, r * mm_rows:(r + 1) * mm_rows, :], w)
        for sl in range(hp_ref.shape[0]):
            hp_ref[sl, CONV_PAD_ROWS + r * mm_rows:CONV_PAD_ROWS + (r + 1) * mm_rows, :] = (
                res[:, sl * LANES:(sl + 1) * LANES])


def _conv_phases(hp_ref, sl, cw_ref, b_ref, row0, rows, taps, lanes):
    pad = taps // 2
    n = rows // CONV_PHASES
    shifted = {m: hp_ref[sl, pl.ds(CONV_PAD_ROWS + row0 + m, n, stride=CONV_PHASES), :]
               for m in range(-pad, CONV_PHASES + pad)}
    outs = []
    for j in range(CONV_PHASES):
        acc = b_ref[:, lanes]
        for k in range(taps):
            acc = acc + cw_ref[k:k + 1, lanes] * shifted[j + k - pad]
        outs.append(acc)
    return outs


def _store_phases(st_ref, sl, row0, rows, phases):
    for j, v in enumerate(phases):
        st_ref[sl, pl.ds(row0 + j, rows // CONV_PHASES, stride=CONV_PHASES), :] = v


def _xbc_kernel(x_ref, w_ref, cw_ref, b_ref, o_ref, *scratch, seq, taps, sub_cols):
    n_sub = len(scratch) // 2
    slabs = sub_cols // LANES
    mm_rows = min(512, seq)
    rows = min(256, seq)
    n_mm = seq // mm_rows
    ws = [w_ref[0, q * sub_cols:(q + 1) * sub_cols, :].astype(BF16) for q in range(n_sub)]

    def conv_block(q, r):
        hp_ref, st_ref = scratch[2 * q], scratch[2 * q + 1]
        for sl in range(slabs):
            gsl = q * slabs + sl
            lanes = slice(gsl * LANES, (gsl + 1) * LANES)
            phases = _conv_phases(hp_ref, sl, cw_ref, b_ref, r * rows, rows, taps, lanes)
            _store_phases(st_ref, sl, r * rows, rows, [_silu_tanh(v) for v in phases])
            o_ref[0, r * rows:(r + 1) * rows, lanes] = (
                st_ref[sl, r * rows:(r + 1) * rows, :].astype(o_ref.dtype))

    _fill_padded(scratch[0], x_ref, ws[0], seq, mm_rows, range(n_mm), transposed_w=True)
    for q in range(n_sub):
        for m in range(n_mm):
            if q + 1 < n_sub:
                _fill_padded(scratch[2 * q + 2], x_ref, ws[q + 1], seq, mm_rows, [m], transposed_w=True)
            for r in range(m * mm_rows // rows, (m + 1) * mm_rows // rows):
                conv_block(q, r)


def _xbc_call(xb, w_in_t, layer, col0, conv_w, conv_b, tn):
    bsz, s, d = xb.shape
    n = conv_w.shape[1]
    taps = conv_w.shape[0]
    sub_cols = 4 * LANES
    sub_scratch = [pltpu.VMEM((sub_cols // LANES, s + 2 * CONV_PAD_ROWS, LANES), F32),
                   pltpu.VMEM((sub_cols // LANES, s, LANES), F32)]
    return pl.pallas_call(
        functools.partial(_xbc_kernel, seq=s, taps=taps, sub_cols=sub_cols),
        grid=(bsz, n // tn),
        in_specs=[pl.BlockSpec((1, s, d), lambda b, j: (b, 0, 0)),
                  pl.BlockSpec((1, tn, d), lambda b, j: (layer, col0 // tn + j, 0)),
                  pl.BlockSpec((taps, tn), lambda b, j: (0, j)),
                  pl.BlockSpec((1, tn), lambda b, j: (0, j))],
        out_specs=pl.BlockSpec((1, s, tn), lambda b, j: (b, 0, j)),
        out_shape=jax.ShapeDtypeStruct((bsz, s, n), BF16),
        scratch_shapes=sub_scratch * (tn // sub_cols),
        compiler_params=_cparams("parallel", "arbitrary"),
        name="xbc",
    )(xb, w_in_t, conv_w, conv_b)


def _ssd_kernel(x_ref, b_ref, c_ref, zs_ref, rows_ref, acol_ref, colfac_ref, e_ref,
                dskip_ref, gain_ref, o_ref, y_ref, xwf_ref, xwb_ref, scf_ref, scb_ref,
                etf_ref, etb_ref, sf_ref, sb_ref, *, n_chunks):
    L = SSD_CHUNK
    HP = HEAD_GROUP_LANES
    HPG = SSD_HEADS_PER_GROUP
    li = lax.broadcasted_iota(jnp.int32, (L, L), 0)
    si = lax.broadcasted_iota(jnp.int32, (L, L), 1)
    lower = si < li
    upper = si > li
    lane = lax.broadcasted_iota(jnp.int32, (L, 2 * SSD_HEAD_DIM), 1)
    first_head = lane < SSD_HEAD_DIM

    def chunk_local(c):
        rc = pl.ds(pl.multiple_of(c * L, L), L)
        xc = x_ref[0, rc, :]
        bc = b_ref[0, rc, :]
        cc = c_ref[0, rc, :]
        scores = lax.dot_general(cc, bc, (((1,), (1,)), ((), ())), preferred_element_type=F32)
        arow = rows_ref[0, c, 0]
        drow = rows_ref[0, c, 1]
        acol = acol_ref[0, 0, rc, :]
        ypairs = []
        for p in range(HPG // 2):
            dmats = []
            for h in (2 * p, 2 * p + 1):
                hb = HPG + h
                seg = jnp.where(lower, acol[:, h:h + 1] - arow[h:h + 1, :],
                                jnp.where(upper, acol[:, hb:hb + 1] - arow[hb:hb + 1, :],
                                          drow[h:h + 1, :]))
                dmats.append((scores * jnp.exp(seg)).astype(BF16))
            dpair = jnp.concatenate(dmats, axis=1)
            xpair = xc[:, p * 2 * SSD_HEAD_DIM:(p + 1) * 2 * SSD_HEAD_DIM]
            zero = jnp.zeros_like(xpair)
            xbd = jnp.concatenate([jnp.where(first_head, xpair, zero),
                                   jnp.where(first_head, zero, xpair)], axis=0)
            ypairs.append(_dot(dpair, xbd))
        xf = xc.astype(F32)
        y_ref[rc, :] = jnp.concatenate(ypairs, axis=1) + dskip_ref[...] * xf

        fac = _dot(colfac_ref[0, 0, rc, :], e_ref[...])
        scf_ref[rc, :] = fac[:, 0 * HP:1 * HP]
        xwf_ref[rc, :] = (xf * fac[:, 1 * HP:2 * HP]).astype(BF16)
        scb_ref[rc, :] = fac[:, 2 * HP:3 * HP]
        xwb_ref[rc, :] = (xf * fac[:, 3 * HP:4 * HP]).astype(BF16)
        etf_ref[c] = jnp.broadcast_to(fac[L - 1:L, 0 * HP:1 * HP], (SUBLANES, HP))
        etb_ref[c] = jnp.broadcast_to(fac[0:1, 2 * HP:3 * HP], (SUBLANES, HP))

    local_unroll = LOCAL_UNROLL if n_chunks % LOCAL_UNROLL == 0 else 1

    def local_step(i, carry):
        for u in range(local_unroll):
            chunk_local(i * local_unroll + u)
        return carry

    lax.fori_loop(0, n_chunks // local_unroll, local_step, 0)

    sf_ref[...] = jnp.zeros(sf_ref.shape, F32)
    sb_ref[...] = jnp.zeros(sb_ref.shape, F32)

    def scan_one(c, s_ref, xw_ref, sc_ref, et_ref):
        rc = pl.ds(pl.multiple_of(c * L, L), L)
        s_old = s_ref[...]
        y_ref[rc, :] += _dot(c_ref[0, rc, :], s_old.astype(BF16)) * sc_ref[rc, :]
        upd = lax.dot_general(b_ref[0, rc, :], xw_ref[rc, :], (((0,), (0,)), ((), ())),
                              preferred_element_type=F32)
        s_ref[...] = s_old * et_ref[c][0:1, :] + upd

    def scan_step(i, carry):
        scan_one(i, sf_ref, xwf_ref, scf_ref, etf_ref)
        scan_one(n_chunks - 1 - i, sb_ref, xwb_ref, scb_ref, etb_ref)
        return carry

    lax.fori_loop(0, n_chunks, scan_step, 0, unroll=SCAN_UNROLL if n_chunks % SCAN_UNROLL == 0 else 1)

    rows = 256 if (n_chunks * L) % 256 == 0 else L
    for r in range(n_chunks * L // rows):
        sl = slice(r * rows, (r + 1) * rows)
        v = y_ref[sl, :] * zs_ref[0, sl, :].astype(F32)
        v = v * lax.rsqrt(jnp.mean(v * v, axis=-1, keepdims=True) + RMS_EPS)
        o_ref[0, sl, :] = (v * gain_ref[...]).astype(o_ref.dtype)


def _ssd_call(xbc, zs, rows, acol, colfac, e_all, dskip, gain):
    bsz, s, _ = xbc.shape
    nc = s // SSD_CHUNK
    HP = HEAD_GROUP_LANES
    n_x_blocks = SSD_HEADS * SSD_HEAD_DIM // SSD_STATE
    return pl.pallas_call(
        functools.partial(_ssd_kernel, n_chunks=nc),
        grid=(bsz, SSD_GROUPS),
        in_specs=[pl.BlockSpec((1, s, HP), lambda b, g: (b, 0, g)),
                  pl.BlockSpec((1, s, SSD_STATE), lambda b, g: (b, 0, n_x_blocks + g)),
                  pl.BlockSpec((1, s, SSD_STATE), lambda b, g: (b, 0, n_x_blocks + SSD_GROUPS + g)),
                  pl.BlockSpec((1, s, HP), lambda b, g: (b, 0, g)),
                  pl.BlockSpec((1, nc, 2, DIRS_X_HEADS, SSD_CHUNK), lambda b, g: (b, 0, 0, g, 0)),
                  pl.BlockSpec((1, 1, s, LANES), lambda b, g: (b, g, 0, 0)),
                  pl.BlockSpec((1, 1, s, LANES), lambda b, g: (b, g, 0, 0)),
                  pl.BlockSpec((LANES, 4 * HP), lambda b, g: (0, 0)),
                  pl.BlockSpec((1, HP), lambda b, g: (0, g)),
                  pl.BlockSpec((1, HP), lambda b, g: (0, g))],
        out_specs=pl.BlockSpec((1, s, HP), lambda b, g: (b, 0, g)),
        out_shape=jax.ShapeDtypeStruct((bsz, s, SSD_GROUPS * HP), BF16),
        scratch_shapes=[pltpu.VMEM((s, HP), F32),
                        pltpu.VMEM((s, HP), BF16), pltpu.VMEM((s, HP), BF16),
                        pltpu.VMEM((s, HP), F32), pltpu.VMEM((s, HP), F32),
                        pltpu.VMEM((nc, SUBLANES, HP), F32), pltpu.VMEM((nc, SUBLANES, HP), F32),
                        pltpu.VMEM((SSD_STATE, HP), F32), pltpu.VMEM((SSD_STATE, HP), F32)],
        compiler_params=_cparams("parallel", "arbitrary"),
        name="ssd",
    )(xbc, xbc, xbc, zs, rows, acol, colfac, e_all, dskip, gain)


def _merge_kernel(yn_ref, pool_ref, g0_ref, g1_ref, x_ref, wsp32_ref, wout32_ref, lng_ref, lnb_ref,
                  of_ref, ob_ref, wsp_ref, wout_ref, *, alpha, sub_rows):
    @pl.when(pl.program_id(0) == 0)
    def _():
        wsp_ref[...] = wsp32_ref[0].astype(BF16)
        wout_ref[...] = wout32_ref[0].astype(BF16)

    for r in range(x_ref.shape[0] // sub_rows):
        rows = slice(r * sub_rows, (r + 1) * sub_rows)
        ssd_out = _dot(yn_ref[rows, :], wsp_ref[...])
        merged = (g0_ref[rows, :].astype(F32) * pool_ref[rows, :].astype(F32)
                  + g1_ref[rows, :].astype(F32) * ssd_out)
        mix = _dot(merged.astype(BF16), wout_ref[...])
        out = _layer_norm(alpha * x_ref[rows, :] + mix, lng_ref[...], lnb_ref[...])
        of_ref[rows, :] = out
        ob_ref[rows, :] = out.astype(BF16)


def _merge_call(yn, pool_out, gates, x, wsp, wout, layer, lng, lnb, alpha, tm):
    t, d = x.shape
    inner = yn.shape[1]
    row = lambda i: (i, 0)
    const = lambda i: (0, 0)
    resident = lambda shape: pl.BlockSpec(shape, lambda i: (layer, 0, 0), pipeline_mode=pl.Buffered(1))
    return pl.pallas_call(
        functools.partial(_merge_kernel, alpha=alpha, sub_rows=min(ROW_SUB_BLOCK, tm)),
        grid=(t // tm,),
        in_specs=[pl.BlockSpec((tm, inner), row),
                  pl.BlockSpec((tm, d), row),
                  pl.BlockSpec((tm, d), lambda i: (i, 0)),
                  pl.BlockSpec((tm, d), lambda i: (i, 1)),
                  pl.BlockSpec((tm, d), row),
                  resident((1, inner, d)),
                  resident((1, d, d)),
                  pl.BlockSpec((1, d), const),
                  pl.BlockSpec((1, d), const)],
        out_specs=[pl.BlockSpec((tm, d), row), pl.BlockSpec((tm, d), row)],
        out_shape=[jax.ShapeDtypeStruct((t, d), F32), jax.ShapeDtypeStruct((t, d), BF16)],
        scratch_shapes=[pltpu.VMEM((inner, d), BF16), pltpu.VMEM((d, d), BF16)],
        compiler_params=_cparams("arbitrary"),
        name="merge",
    )(yn, pool_out, gates, gates, x, wsp, wout, lng, lnb)


def _ffn_up_kernel(x_ref, wg_ref, wv_ref, cwg_ref, cwv_ref, bg_ref, bv_ref, o_ref, hg_ref, hv_ref,
                   st_ref, *, seq, taps):
    mm_rows = min(512, seq)
    _fill_padded(hg_ref, x_ref, wg_ref[0].astype(BF16), seq, mm_rows, range(seq // mm_rows))
    _fill_padded(hv_ref, x_ref, wv_ref[0].astype(BF16), seq, mm_rows, range(seq // mm_rows))
    rows = min(256, seq)
    half_cwv = 0.5 * cwv_ref[...]
    half_bv = 0.5 * bv_ref[...]
    for r in range(seq // rows):
        for sl in range(hg_ref.shape[0]):
            lanes = slice(sl * LANES, (sl + 1) * LANES)
            gates = _conv_phases(hg_ref, sl, cwg_ref, bg_ref, r * rows, rows, taps, lanes)
            half_vals = _conv_phases(hv_ref, sl, half_cwv, half_bv, r * rows, rows, taps, lanes)
            acts = [g * (1.0 + lax.erf(g * np.float32(2.0 ** -0.5))) * hv
                    for g, hv in zip(gates, half_vals)]
            _store_phases(st_ref, sl, r * rows, rows, acts)
            o_ref[0, r * rows:(r + 1) * rows, sl * LANES:(sl + 1) * LANES] = (
                st_ref[sl, r * rows:(r + 1) * rows, :].astype(o_ref.dtype))


def _ffn_up_call(xb, w_up, layer, conv_w, conv_b, tn):
    bsz, s, d = xb.shape
    dff = w_up.shape[2] // 2
    nt = dff // tn
    taps = conv_w.shape[0]
    return pl.pallas_call(
        functools.partial(_ffn_up_kernel, seq=s, taps=taps),
        grid=(bsz, nt),
        in_specs=[pl.BlockSpec((1, s, d), lambda b, j: (b, 0, 0)),
                  pl.BlockSpec((1, d, tn), lambda b, j: (layer, 0, j)),
                  pl.BlockSpec((1, d, tn), lambda b, j: (layer, 0, nt + j)),
                  pl.BlockSpec((taps, tn), lambda b, j: (0, j)),
                  pl.BlockSpec((taps, tn), lambda b, j: (0, nt + j)),
                  pl.BlockSpec((1, tn), lambda b, j: (0, j)),
                  pl.BlockSpec((1, tn), lambda b, j: (0, nt + j))],
        out_specs=pl.BlockSpec((1, s, tn), lambda b, j: (b, 0, j)),
        out_shape=jax.ShapeDtypeStruct((bsz, s, dff), BF16),
        scratch_shapes=[pltpu.VMEM((tn // LANES, s + 2 * CONV_PAD_ROWS, LANES), F32),
                        pltpu.VMEM((tn // LANES, s + 2 * CONV_PAD_ROWS, LANES), F32),
                        pltpu.VMEM((tn // LANES, s, LANES), F32)],
        compiler_params=_cparams("parallel", "arbitrary"),
        name="ffn_up",
    )(xb, w_up, w_up, conv_w, conv_w, conv_b, conv_b)


def _ffn_down_kernel(a_ref, x_ref, w32_ref, lng_ref, lnb_ref, of_ref, ob_ref, w_ref, *, alpha, sub_rows):
    @pl.when(pl.program_id(0) == 0)
    def _():
        w_ref[...] = w32_ref[0].astype(BF16)

    for r in range(x_ref.shape[0] // sub_rows):
        rows = slice(r * sub_rows, (r + 1) * sub_rows)
        out = _layer_norm(alpha * x_ref[rows, :] + _dot(a_ref[rows, :], w_ref[...]),
                          lng_ref[...], lnb_ref[...])
        of_ref[rows, :] = out
        ob_ref[rows, :] = out.astype(BF16)


def _ffn_down_call(act, x, w_down, layer, lng, lnb, alpha, tm):
    t, d = x.shape
    dff = act.shape[1]
    row = lambda i: (i, 0)
    const = lambda i: (0, 0)
    return pl.pallas_call(
        functools.partial(_ffn_down_kernel, alpha=alpha, sub_rows=min(ROW_SUB_BLOCK, tm)),
        grid=(t // tm,),
        in_specs=[pl.BlockSpec((tm, dff), row), pl.BlockSpec((tm, d), row),
                  pl.BlockSpec((1, dff, d), lambda i: (layer, 0, 0), pipeline_mode=pl.Buffered(1)),
                  pl.BlockSpec((1, d), const), pl.BlockSpec((1, d), const)],
        out_specs=[pl.BlockSpec((tm, d), row), pl.BlockSpec((tm, d), row)],
        out_shape=[jax.ShapeDtypeStruct((t, d), F32), jax.ShapeDtypeStruct((t, d), BF16)],
        scratch_shapes=[pltpu.VMEM((dff, d), BF16)],
        compiler_params=_cparams("arbitrary"),
        name="ffn_down",
    )(act, x, w_down, lng, lnb)


def _placement_constants():
    nh = 2 * SSD_HEADS
    pa = np.zeros((nh, SSD_GROUPS * LANES), np.float32)
    pe = np.zeros((2, nh, SSD_GROUPS * LANES), np.float32)
    pw = np.zeros((2, nh, SSD_GROUPS * LANES), np.float32)
    pm = np.zeros((nh, nh), np.float32)
    for g in range(SSD_GROUPS):
        for j in range(DIRS_X_HEADS):
            direction, head = divmod(j, SSD_HEADS_PER_GROUP)
            col = direction * SSD_HEADS + g * SSD_HEADS_PER_GROUP + head
            pm[g * DIRS_X_HEADS + j, col] = 1.0
            lane0 = g * LANES
            pa[col, lane0 + j] = 1.0
            for part in range(2):
                pe[part, col, lane0 + part * 2 * DIRS_X_HEADS + j] = 1.0
                pw[part, col, lane0 + part * 2 * DIRS_X_HEADS + DIRS_X_HEADS + j] = 1.0
    HP = HEAD_GROUP_LANES
    ef = np.zeros((LANES, 2 * HP), np.float32)
    eb = np.zeros((LANES, 2 * HP), np.float32)
    for part in range(2):
        base = part * 2 * DIRS_X_HEADS
        for h in range(SSD_HEADS_PER_GROUP):
            cols = slice(h * SSD_HEAD_DIM, (h + 1) * SSD_HEAD_DIM)
            cols_w = slice(HP + h * SSD_HEAD_DIM, HP + (h + 1) * SSD_HEAD_DIM)
            ef[base + h, cols] = 1.0
            eb[base + SSD_HEADS_PER_GROUP + h, cols] = 1.0
            ef[base + DIRS_X_HEADS + h, cols_w] = 1.0
            eb[base + DIRS_X_HEADS + SSD_HEADS_PER_GROUP + h, cols_w] = 1.0
    as_bf16 = lambda a: jnp.asarray(a, BF16)
    return as_bf16(pa), as_bf16(pe), as_bf16(pw), as_bf16(pm), as_bf16(np.concatenate([ef, eb], axis=1))


def kernel(x, w_in, pool_w, pool_scale, ssd_conv_w, ssd_conv_b, a_log, dt_bias, d_skip, ssd_norm_g,
           w_ssd_proj, w_out, ln1_g, ln1_b, w_up, ffn_conv_w, ffn_conv_b, w_down, ln2_g, ln2_b):
    bsz, s, d = x.shape
    depth = w_in.shape[0]
    t = bsz * s
    alpha = float((2 * depth) ** 0.25)
    pool_width = pool_w.shape[1] * pool_w.shape[2]
    inner = w_ssd_proj.shape[1]
    conv_ch = ssd_conv_w.shape[2]
    n_dt = 2 * SSD_HEADS
    c0, c1, c2, c3 = pool_width, pool_width + inner, pool_width + inner + conv_ch, \
        pool_width + inner + conv_ch + n_dt
    tm = min(1024, t)
    pa, pe, pw, pm, e_all = _placement_constants()

    w_in_t = jnp.swapaxes(w_in, 1, 2)
    xf = x.reshape(t, d)
    xb3 = x.astype(BF16)
    for i in range(depth):
        w_gate_t = w_in_t[i, c3:][None]
        rows, acol, colfac = _dt_call(xb3, w_in_t, i, c2, dt_bias[i].reshape(1, -1),
                                      a_log[i].reshape(1, -1), pa, pe, pw, pm)
        pool_out = _pool_call(xb3, w_in_t, i, pool_w, pool_scale[i].reshape(1, -1))
        zs = _act_mm_call(xb3, w_in_t, i, c0, inner, True, 1024, "z")
        gates = _act_mm_call(xb3, w_gate_t, 0, 0, 2 * d, False, 1024, "gates").reshape(t, 2 * d)
        xbc = _xbc_call(xb3, w_in_t, i, c1, ssd_conv_w[i], ssd_conv_b[i].reshape(1, -1), 1024)
        yn = _ssd_call(xbc, zs, rows, acol, colfac, e_all,
                       jnp.repeat(d_skip[i], SSD_HEAD_DIM).reshape(1, -1),
                       ssd_norm_g[i].reshape(1, -1))
        xf, xb = _merge_call(yn.reshape(t, inner), pool_out.reshape(t, d), gates, xf,
                             w_ssd_proj, w_out, i, ln1_g[i].reshape(1, -1), ln1_b[i].reshape(1, -1),
                             alpha, min(512, t))
        act = _ffn_up_call(xb.reshape(bsz, s, d), w_up, i, ffn_conv_w[i],
                           ffn_conv_b[i].reshape(1, -1), 256)
        xf, xb = _ffn_down_call(act.reshape(t, -1), xf, w_down, i,
                                ln2_g[i].reshape(1, -1), ln2_b[i].reshape(1, -1), alpha, tm)
        xb3 = xb.reshape(bsz, s, d)
    return xf.reshape(bsz, s, d)
```
